```python
import jax, jax.numpy as jnp
from jax import lax
import numpy as np

D_MODEL = 1024
BATCH = 16
SEQ = 2048
DEPTH = 2

NORM_EPS = 1e-6
CONV_WIDTH = 4
DN_HEADS = 4
DN_HEAD_DIM = 128
DN_WIDTH = DN_HEADS * DN_HEAD_DIM
DN_CHUNK = 64
MB_HEADS = 4
MB_HEAD_DIM = 128
MB_WIDTH = MB_HEADS * MB_HEAD_DIM
MB_BLOCK = 256
MB_TOP_K = 3
MB_QUERY_BLOCK = 16
MB_SCALE = MB_HEAD_DIM ** -0.5
ROPE_THETA = 500000.0
ROPE_DIMS = MB_HEAD_DIM // 4
LRU_WIDTH = 512
LRU_GROUPS = 8
LRU_GROUP_DIM = LRU_WIDTH // LRU_GROUPS
LRU_C = 8.0
N_BRANCH = 3
FFN_HIDDEN = -(-8 * D_MODEL // (3 * 256)) * 256
IN_SPLIT_WIDTHS = (DN_WIDTH, DN_WIDTH, DN_WIDTH, DN_WIDTH, DN_HEADS, DN_HEADS,
                   MB_WIDTH, MB_WIDTH, MB_WIDTH, LRU_WIDTH, LRU_WIDTH, N_BRANCH * D_MODEL)
IN_COLS = sum(IN_SPLIT_WIDTHS)

kernel_name = 'hybrid_deltanet_moba_rglru_block'


def rms_norm(x, w):
    xf = x.astype(jnp.float32)
    y = xf * lax.rsqrt(jnp.mean(xf * xf, axis=-1, keepdims=True) + NORM_EPS)
    return (y * w.astype(jnp.float32)).astype(x.dtype)


def l2_normalize(x):
    return x * lax.rsqrt(jnp.sum(x * x, axis=-1, keepdims=True) + NORM_EPS)


def split_columns(p):
    out = []
    start = 0
    for w in IN_SPLIT_WIDTHS:
        out.append(p[..., start:start + w])
        start += w
    return out


def causal_depthwise_conv(x, w):
    width, ch = w.shape
    return lax.conv_general_dilated(
        x, w[:, None, :].astype(x.dtype), window_strides=(1,), padding=[(width - 1, 0)],
        dimension_numbers=('NWC', 'WIO', 'NWC'), feature_group_count=ch)


def partial_rope(x):
    t = x.shape[1]
    half = ROPE_DIMS // 2
    inv_freq = ROPE_THETA ** (-jnp.arange(half, dtype=jnp.float32) / half)
    ang = jnp.arange(t, dtype=jnp.float32)[:, None] * inv_freq[None, :]
    cos = jnp.cos(ang)[None, :, None, :]
    sin = jnp.sin(ang)[None, :, None, :]
    xr = x[..., :ROPE_DIMS].astype(jnp.float32)
    x1, x2 = xr[..., :half], xr[..., half:]
    rot = jnp.concatenate([x1 * cos - x2 * sin, x2 * cos + x1 * sin], axis=-1).astype(x.dtype)
    return jnp.concatenate([rot, x[..., ROPE_DIMS:]], axis=-1)


def chunk_gated_delta_rule(q, k, v, g, beta):
    b, h, t, dk = q.shape
    dv = v.shape[-1]
    c = DN_CHUNK
    n = t // c
    q = q.reshape(b, h, n, c, dk)
    k = k.reshape(b, h, n, c, dk)
    v = v.reshape(b, h, n, c, dv)
    g = jnp.cumsum(g.reshape(b, h, n, c), axis=-1)
    beta = beta.reshape(b, h, n, c)
    k_beta = k * beta[..., None]
    v_beta = v * beta[..., None]
    incl = jnp.tril(jnp.ones((c, c), dtype=bool))
    strict = jnp.tril(jnp.ones((c, c), dtype=bool), k=-1)
    decay = jnp.exp(jnp.where(incl, g[..., :, None] - g[..., None, :], -jnp.inf))
    lower = jnp.where(strict, jnp.einsum('bhnid,bhnjd->bhnij', k_beta, k) * decay, 0.0)
    eye = jnp.eye(c, dtype=jnp.float32)
    t_inv = lax.linalg.triangular_solve(lower + eye, jnp.broadcast_to(eye, lower.shape),
                                        left_side=True, lower=True, unit_diagonal=True)
    u = jnp.einsum('bhnij,bhnjd->bhnid', t_inv, v_beta)
    w = jnp.einsum('bhnij,bhnjd->bhnid', t_inv, k_beta * jnp.exp(g)[..., None])
    attn = jnp.einsum('bhnid,bhnjd->bhnij', q, k) * decay
    g_last = g[..., -1]
    k_to_end = k * jnp.exp(g_last[..., None] - g)[..., None]
    q_dec = q * jnp.exp(g)[..., None]

    def step(state, xs):
        q_c, k_c, u_c, w_c, a_c, gl = xs
        v_new = u_c - jnp.einsum('bhcd,bhde->bhce', w_c, state)
        o = jnp.einsum('bhcd,bhde->bhce', q_c, state) + jnp.einsum('bhij,bhje->bhie', a_c, v_new)
        state = state * jnp.exp(gl)[..., None, None] + jnp.einsum('bhcd,bhce->bhde', k_c, v_new)
        return state, o

    xs = tuple(jnp.moveaxis(z, 2, 0) for z in (q_dec, k_to_end, u, w, attn, g_last))
    s0 = jnp.zeros((b, h, dk, dv), jnp.float32)
    _, o = lax.scan(step, s0, xs)
    return jnp.moveaxis(o, 0, 2).reshape(b, h, t, dv)


def gated_deltanet(q, k, v, z, a, bb, conv_w, a_log, dt_bias, norm_w):
    b, t, _ = q.shape
    qkv = jax.nn.silu(causal_depthwise_conv(jnp.concatenate([q, k, v], axis=-1), conv_w))
    q, k, v = jnp.split(qkv, 3, axis=-1)

    def heads(y):
        return y.reshape(b, t, DN_HEADS, DN_HEAD_DIM).transpose(0, 2, 1, 3).astype(jnp.float32)

    qh = l2_normalize(heads(q)) * (DN_HEAD_DIM ** -0.5)
    kh = l2_normalize(heads(k))
    vh = heads(v)
    g = -jnp.exp(a_log.astype(jnp.float32)) * jax.nn.softplus(a.astype(jnp.float32) + dt_bias.astype(jnp.float32))
    beta = jax.nn.sigmoid(bb.astype(jnp.float32))
    o = chunk_gated_delta_rule(qh, kh, vh, g.transpose(0, 2, 1), beta.transpose(0, 2, 1))
    o = o.transpose(0, 2, 1, 3)
    zf = z.reshape(b, t, DN_HEADS, DN_HEAD_DIM).astype(jnp.float32)
    o = rms_norm(o, norm_w) * jax.nn.silu(zf)
    return o.reshape(b, t, DN_WIDTH).astype(q.dtype)


def moba_attention(q, k, v):
    b, t, _ = q.shape
    h, dh = MB_HEADS, MB_HEAD_DIM
    q = partial_rope(q.reshape(b, t, h, dh))
    k = partial_rope(k.reshape(b, t, h, dh))
    v = v.reshape(b, t, h, dh)
    nb = -(-t // MB_BLOCK)
    tp = nb * MB_BLOCK
    n_sel = min(MB_TOP_K, nb)
    pad = ((0, 0), (0, tp - t), (0, 0), (0, 0))
    kb = jnp.pad(k, pad).reshape(b, nb, MB_BLOCK, h, dh).transpose(0, 3, 1, 2, 4)
    vb = jnp.pad(v, pad).reshape(b, nb, MB_BLOCK, h, dh).transpose(0, 3, 1, 2, 4)
    qh = q.transpose(0, 2, 1, 3)
    k_mean = jnp.mean(kb.astype(jnp.float32), axis=3)
    q_block = jnp.arange(t) // MB_BLOCK
    fully_past = jnp.arange(nb)[None, :] < q_block[:, None]
    gate = jnp.where(fully_past, jnp.einsum('bhtd,bhnd->bhtn', qh.astype(jnp.float32), k_mean), -jnp.inf)
    _, sel = lax.top_k(gate, n_sel)
    sel_valid = sel < q_block[:, None]
    bi = jnp.arange(b)[:, None, None, None]
    hi = jnp.arange(h)[None, :, None, None]

    def attend_query_block(ci):
        t0 = ci * MB_QUERY_BLOCK
        qc = lax.dynamic_slice_in_dim(qh, t0, MB_QUERY_BLOCK, axis=2)
        selc = lax.dynamic_slice_in_dim(sel, t0, MB_QUERY_BLOCK, axis=2)
        validc = lax.dynamic_slice_in_dim(sel_valid, t0, MB_QUERY_BLOCK, axis=2)
        own = t0 // MB_BLOCK
        k_own = lax.dynamic_index_in_dim(kb, own, axis=2, keepdims=False)
        v_own = lax.dynamic_index_in_dim(vb, own, axis=2, keepdims=False)
        k_sel = kb[bi, hi, selc]
        v_sel = vb[bi, hi, selc]
        q_pos = t0 + jnp.arange(MB_QUERY_BLOCK)
        k_pos = own * MB_BLOCK + jnp.arange(MB_BLOCK)
        s_own = jnp.einsum('bhqd,bhkd->bhqk', qc, k_own, preferred_element_type=jnp.float32) * MB_SCALE
        s_own = jnp.where(k_pos[None, :] <= q_pos[:, None], s_own, -jnp.inf)
        s_sel = jnp.einsum('bhqd,bhqnkd->bhqnk', qc, k_sel, preferred_element_type=jnp.float32) * MB_SCALE
        s_sel = jnp.where(validc[..., None], s_sel, -jnp.inf).reshape(b, h, MB_QUERY_BLOCK, n_sel * MB_BLOCK)
        p = jax.nn.softmax(jnp.concatenate([s_own, s_sel], axis=-1), axis=-1)
        p_own = p[..., :MB_BLOCK].astype(v.dtype)
        p_sel = p[..., MB_BLOCK:].reshape(b, h, MB_QUERY_BLOCK, n_sel, MB_BLOCK).astype(v.dtype)
        return (jnp.einsum('bhqk,bhkd->bhqd', p_own, v_own)
                + jnp.einsum('bhqnk,bhqnkd->bhqd', p_sel, v_sel))

    o = lax.map(attend_query_block, jnp.arange(t // MB_QUERY_BLOCK))
    return o.transpose(1, 0, 3, 2, 4).reshape(b, t, h * dh)


def rg_lru_branch(xb, yb, conv_w, conv_b, w_r, b_r, w_i, b_i, lam):
    b, t, _ = xb.shape
    xc = causal_depthwise_conv(xb, conv_w) + conv_b.astype(xb.dtype)
    xg = xc.reshape(b, t, LRU_GROUPS, LRU_GROUP_DIM)
    r = jax.nn.sigmoid((jnp.einsum('btgi,gij->btgj', xg, w_r).reshape(b, t, LRU_WIDTH) + b_r).astype(jnp.float32))
    i = jax.nn.sigmoid((jnp.einsum('btgi,gij->btgj', xg, w_i).reshape(b, t, LRU_WIDTH) + b_i).astype(jnp.float32))
    log_a = -LRU_C * r * jax.nn.softplus(-lam.astype(jnp.float32))
    a = jnp.exp(log_a)
    u = jnp.sqrt(-jnp.expm1(2.0 * log_a)) * (i * xc.astype(jnp.float32))

    def combine(c1, c2):
        a1, b1 = c1
        a2, b2 = c2
        return a1 * a2, a2 * b1 + b2

    _, hs = lax.associative_scan(combine, (a, u), axis=1)
    return (hs * jax.nn.gelu(yb.astype(jnp.float32))).astype(xb.dtype)


def setup_inputs(seed: int = 0) -> dict:
    key = jax.random.key(seed)
    ks = iter(jax.random.split(key, 40))
    f32 = jnp.float32
    L = DEPTH

    def nrm(shape, fan_in):
        return jax.random.normal(next(ks), shape, f32) * (fan_in ** -0.5)

    def small(shape):
        return 0.02 * jax.random.normal(next(ks), shape, f32)

    x = jax.random.normal(next(ks), (BATCH, SEQ, D_MODEL), f32)
    norm_mix = 1.0 + small((L, D_MODEL))
    w_in = nrm((L, D_MODEL, IN_COLS), D_MODEL)
    b_gate = small((L, N_BRANCH * D_MODEL))
    dn_conv = nrm((L, CONV_WIDTH, 3 * DN_WIDTH), CONV_WIDTH)
    dn_a_log = jnp.log(jax.random.uniform(next(ks), (L, DN_HEADS), f32, 1.0, 16.0))
    dt = jnp.exp(jax.random.uniform(next(ks), (L, DN_HEADS), f32, np.log(1e-3), np.log(1e-1)))
    dn_dt_bias = dt + jnp.log(-jnp.expm1(-dt))
    dn_norm = 1.0 + small((L, DN_HEAD_DIM))
    dn_out = nrm((L, DN_WIDTH, D_MODEL), DN_WIDTH)
    mb_out = nrm((L, MB_WIDTH, D_MODEL), MB_WIDTH)
    lru_conv_w = nrm((L, CONV_WIDTH, LRU_WIDTH), CONV_WIDTH)
    lru_conv_b = small((L, LRU_WIDTH))
    lru_w_r = nrm((L, LRU_GROUPS, LRU_GROUP_DIM, LRU_GROUP_DIM), LRU_GROUP_DIM)
    lru_b_r = small((L, LRU_WIDTH))
    lru_w_i = nrm((L, LRU_GROUPS, LRU_GROUP_DIM, LRU_GROUP_DIM), LRU_GROUP_DIM)
    lru_b_i = small((L, LRU_WIDTH))
    a0 = jax.random.uniform(next(ks), (L, LRU_WIDTH), f32, 0.9, 0.999)
    p = a0 ** (1.0 / LRU_C)
    lru_lambda = jnp.log(p) - jnp.log1p(-p)
    lru_out = nrm((L, LRU_WIDTH, D_MODEL), LRU_WIDTH)
    w_o = nrm((L, D_MODEL, D_MODEL), D_MODEL)
    norm_ffn = 1.0 + small((L, D_MODEL))
    ffn_in = nrm((L, D_MODEL, 2 * FFN_HIDDEN), D_MODEL)
    ffn_down = nrm((L, FFN_HIDDEN, D_MODEL), FFN_HIDDEN)
    norm_final = 1.0 + small((D_MODEL,))
    return {'x': x, 'norm_mix': norm_mix, 'w_in': w_in, 'b_gate': b_gate,
            'dn_conv': dn_conv, 'dn_a_log': dn_a_log, 'dn_dt_bias': dn_dt_bias, 'dn_norm': dn_norm,
            'dn_out': dn_out, 'mb_out': mb_out,
            'lru_conv_w': lru_conv_w, 'lru_conv_b': lru_conv_b, 'lru_w_r': lru_w_r, 'lru_b_r': lru_b_r,
            'lru_w_i': lru_w_i, 'lru_b_i': lru_b_i, 'lru_lambda': lru_lambda, 'lru_out': lru_out,
            'w_o': w_o, 'norm_ffn': norm_ffn, 'ffn_in': ffn_in, 'ffn_down': ffn_down,
            'norm_final': norm_final}


def reference(x, norm_mix, w_in, b_gate, dn_conv, dn_a_log, dn_dt_bias, dn_norm, dn_out, mb_out,
              lru_conv_w, lru_conv_b, lru_w_r, lru_b_r, lru_w_i, lru_b_i, lru_lambda, lru_out,
              w_o, norm_ffn, ffn_in, ffn_down, norm_final):
    b, t, d = x.shape
    for l in range(DEPTH):
        h = rms_norm(x, norm_mix[l])
        proj = h @ w_in[l]
        (dq, dk, dv, dz, da, db, mq, mk, mv, lx, ly, gates) = split_columns(proj)
        y_dn = gated_deltanet(dq, dk, dv, dz, da, db, dn_conv[l], dn_a_log[l], dn_dt_bias[l], dn_norm[l]) @ dn_out[l]
        y_mb = moba_attention(mq, mk, mv) @ mb_out[l]
        y_lru = rg_lru_branch(lx, ly, lru_conv_w[l], lru_conv_b[l], lru_w_r[l], lru_b_r[l],
                              lru_w_i[l], lru_b_i[l], lru_lambda[l]) @ lru_out[l]
        g = jax.nn.sigmoid((gates + b_gate[l]).astype(jnp.float32)).reshape(b, t, N_BRANCH, d)
        mixed = (g[:, :, 0] * y_dn + g[:, :, 1] * y_mb + g[:, :, 2] * y_lru).astype(x.dtype)
        x = x + mixed @ w_o[l]
        h = rms_norm(x, norm_ffn[l])
        gate, up = jnp.split(h @ ffn_in[l], 2, axis=-1)
        x = x + (jax.nn.silu(gate) * up) @ ffn_down[l]
    return rms_norm(x, norm_final)
```

```python
import functools

import jax
import jax.numpy as jnp
from jax import lax
from jax.experimental import pallas as pl
from jax.experimental.pallas import tpu as pltpu

F32 = jnp.float32
BF16 = jnp.bfloat16
HIGHEST = lax.Precision.HIGHEST

D_MODEL = 1024
NORM_EPS = 1e-6
CONV_WIDTH = 4
DN_HEADS = 4
DN_HEAD_DIM = 128
DN_WIDTH = DN_HEADS * DN_HEAD_DIM
MB_HEADS = 4
MB_HEAD_DIM = 128
MB_WIDTH = MB_HEADS * MB_HEAD_DIM
MB_BLOCK = 256
MB_TOP_K = 3
MB_SCALE = MB_HEAD_DIM ** -0.5
ROPE_THETA = 500000.0
ROPE_DIMS = MB_HEAD_DIM // 4
LRU_WIDTH = 512
LRU_GROUPS = 8
LRU_GROUP_DIM = LRU_WIDTH // LRU_GROUPS
LRU_C = 8.0
N_BRANCH = 3
FFN_HIDDEN = -(-8 * D_MODEL // (3 * 256)) * 256

LANES = 128
SUBLANES = 8
VMEM_LIMIT = 56 * 1024 * 1024

OFF_GATES = 0
OFF_DQ = OFF_GATES + N_BRANCH * D_MODEL
OFF_DK = OFF_DQ + DN_WIDTH
OFF_DV = OFF_DK + DN_WIDTH
OFF_DZ = OFF_DV + DN_WIDTH
OFF_MQ = OFF_DZ + DN_WIDTH
OFF_MK = OFF_MQ + MB_WIDTH
OFF_MV = OFF_MK + MB_WIDTH
OFF_LX = OFF_MV + MB_WIDTH
OFF_LY = OFF_LX + LRU_WIDTH
PROJ_COLS = OFF_LY + LRU_WIDTH

DN_CHUNK = 128
DN_INV_BLOCK = 16
DN_GROUP = 4


def _params(sem):
    return pltpu.CompilerParams(dimension_semantics=sem, vmem_limit_bytes=VMEM_LIMIT)


def _sigmoid(x):
    return 1.0 / (1.0 + jnp.exp(-x))


def _silu(x):
    return x * _sigmoid(x)


def _softplus(x):
    return jnp.maximum(x, 0.0) + jnp.log1p(jnp.exp(-jnp.abs(x)))


def _gelu_tanh(x):
    c = (2.0 / jnp.pi) ** 0.5
    return 0.5 * x * (1.0 + jnp.tanh(c * (x + 0.044715 * (x * x * x))))


def _bdot(a, b):
    return jnp.dot(a.astype(BF16), b.astype(BF16), preferred_element_type=F32)


def _bdot_nt(a, b):
    return lax.dot_general(a.astype(BF16), b.astype(BF16), (((1,), (1,)), ((), ())),
                           preferred_element_type=F32)


def _bdot_tn(a, b):
    return lax.dot_general(a.astype(BF16), b.astype(BF16), (((0,), (0,)), ((), ())),
                           preferred_element_type=F32)


def _causal_conv(cur, prev, w):
    ext = jnp.concatenate([prev, cur], axis=0)
    out = cur * w[CONV_WIDTH - 1:CONV_WIDTH, :]
    for s in range(1, CONV_WIDTH):
        shifted = pltpu.roll(ext, s, axis=0)[SUBLANES:, :]
        out = out + shifted * w[CONV_WIDTH - 1 - s:CONV_WIDTH - s, :]
    return out


def _prev_rows(ref, r0):
    start = pl.multiple_of(jnp.maximum(r0 - SUBLANES, 0), SUBLANES)
    rows = ref[pl.ds(start, SUBLANES), :].astype(F32)
    return jnp.where(r0 > 0, rows, 0.0)


def _norm_proj_kernel(x_ref, g_ref, w_ref, wab_ref, o_ref, oab_ref, hn_ref):
    @pl.when(pl.program_id(1) == 0)
    def _():
        x = x_ref[...]
        ms = jnp.mean(x * x, axis=-1, keepdims=True)
        hn = (x * lax.rsqrt(ms + NORM_EPS) * g_ref[...]).astype(BF16)
        hn_ref[...] = hn
        oab_ref[...] = jnp.dot(hn, wab_ref[...], preferred_element_type=F32)

    o_ref[...] = jnp.dot(hn_ref[...], w_ref[...], preferred_element_type=F32).astype(o_ref.dtype)


def _norm_proj(x2, gain, w_main, w_ab, tm, tn):
    n, d = x2.shape
    cols = w_main.shape[1]
    return pl.pallas_call(
        _norm_proj_kernel,
        grid=(n // tm, cols // tn),
        in_specs=[
            pl.BlockSpec((tm, d), lambda i, j: (i, 0)),
            pl.BlockSpec((1, d), lambda i, j: (0, 0)),
            pl.BlockSpec((d, tn), lambda i, j: (0, j)),
            pl.BlockSpec((d, LANES), lambda i, j: (0, 0)),
        ],
        out_specs=[
            pl.BlockSpec((tm, tn), lambda i, j: (i, j)),
            pl.BlockSpec((tm, LANES), lambda i, j: (i, 0)),
        ],
        out_shape=[
            jax.ShapeDtypeStruct((n, cols), F32),
            jax.ShapeDtypeStruct((n, LANES), F32),
        ],
        scratch_shapes=[pltpu.VMEM((tm, d), BF16)],
        compiler_params=_params(("parallel", "arbitrary")),
        name="norm_proj",
    )(x2, gain, w_main, w_ab)


def _unit_lower_inverse(low, row, col):
    c = low.shape[0]
    same_blk = (row // DN_INV_BLOCK) == (col // DN_INV_BLOCK)
    eye = jnp.where(row == col, 1.0, 0.0).astype(F32)
    dg = jnp.where(same_blk, low, 0.0)
    off = low - dg
    inv_d = eye - dg
    pw = dg
    steps = DN_INV_BLOCK.bit_length() - 1
    for _ in range(steps - 1):
        pw = _bdot(pw, pw)
        inv_d = inv_d + _bdot(inv_d, pw)
    m = _bdot(inv_d, off)
    inv_m = eye - m
    pw = m
    steps = (c // DN_INV_BLOCK).bit_length() - 1
    for _ in range(steps - 1):
        pw = _bdot(pw, pw)
        inv_m = inv_m + _bdot(inv_m, pw)
    return _bdot(inv_m, inv_d)


def _dn_kernel(q_ref, k_ref, v_ref, z_ref, ab_ref, cwq_ref, cwk_ref, cwv_ref, alog_ref, dtb_ref, nw_ref,
               o_ref, u_s, w_s, at_s, qd_s, ke_s, egl_s):
    t = q_ref.shape[0]
    c = DN_CHUNK
    nc = t // c
    grp = min(DN_GROUP, nc)
    rows_g = grp * c
    h = pl.program_id(1)

    row = lax.broadcasted_iota(jnp.int32, (c, c), 0)
    col = lax.broadcasted_iota(jnp.int32, (c, c), 1)
    tril = jnp.where(row >= col, 1.0, 0.0).astype(F32)
    sel_rows = lax.broadcasted_iota(jnp.int32, (LANES, LANES), 0)
    pick_a = jnp.where(sel_rows == h, 1.0, 0.0).astype(F32)
    pick_b = jnp.where(sel_rows == h + DN_HEADS, 1.0, 0.0).astype(F32)
    neg_a = -jnp.exp(alog_ref[0])
    dtb = dtb_ref[0]

    def conv_silu(ref, cw_ref, r0):
        cur = ref[pl.ds(r0, rows_g), :].astype(F32)
        return _silu(_causal_conv(cur, _prev_rows(ref, r0), cw_ref[...]))

    def l2n(x):
        return x * lax.rsqrt(jnp.sum(x * x, axis=-1, keepdims=True) + NORM_EPS)

    def prep(gi, carry):
        r0 = pl.multiple_of(gi * rows_g, rows_g)
        qg = l2n(conv_silu(q_ref, cwq_ref, r0)) * (DN_HEAD_DIM ** -0.5)
        kg = l2n(conv_silu(k_ref, cwk_ref, r0))
        vg = conv_silu(v_ref, cwv_ref, r0)
        ab = ab_ref[pl.ds(r0, rows_g), :]
        a_b = jnp.dot(ab, pick_a, precision=HIGHEST, preferred_element_type=F32)
        b_b = jnp.dot(ab, pick_b, precision=HIGHEST, preferred_element_type=F32)
        g_b = neg_a * _softplus(a_b + dtb)
        beta_b = _sigmoid(b_b)
        for ci in range(grp):
            sl = slice(ci * c, (ci + 1) * c)
            qc, kc, vc = qg[sl], kg[sl], vg[sl]
            beta = beta_b[sl]
            gcb = jnp.dot(tril, g_b[sl], precision=HIGHEST, preferred_element_type=F32)
            gcr = gcb.T
            decay = jnp.exp(jnp.where(row >= col, gcb - gcr, -jnp.inf))
            kb = kc * beta
            low = jnp.where(row > col, _bdot_nt(kb, kc) * decay, 0.0)
            t_inv = _unit_lower_inverse(low, row, col)
            eg = jnp.exp(gcb)
            u = _bdot(t_inv, vc * beta)
            w = _bdot(t_inv, kb * eg)
            attn = _bdot_nt(qc, kc) * decay
            gl = gcb[c - 1:c, :]
            rs = pl.ds(pl.multiple_of(r0 + ci * c, c), c)
            u_s[rs, :] = u
            w_s[rs, :] = w.astype(BF16)
            at_s[rs, :] = attn.astype(BF16)
            qd_s[rs, :] = (qc * eg).astype(BF16)
            ke_s[rs, :] = (kc * jnp.exp(gl - gcb)).astype(BF16)
            es = pl.ds(pl.multiple_of((gi * grp + ci) * SUBLANES, SUBLANES), SUBLANES)
            egl_s[es, :] = jnp.broadcast_to(jnp.exp(gl), (SUBLANES, LANES))
        return carry

    lax.fori_loop(0, nc // grp, prep, 0)

    nw = nw_ref[...]

    def step(ci, state):
        rs = pl.ds(pl.multiple_of(ci * c, c), c)
        sb = state.astype(BF16)
        v_new = u_s[rs, :] - jnp.dot(w_s[rs, :], sb, preferred_element_type=F32)
        vb = v_new.astype(BF16)
        o = (jnp.dot(qd_s[rs, :], sb, preferred_element_type=F32)
             + jnp.dot(at_s[rs, :], vb, preferred_element_type=F32))
        egl = egl_s[pl.ds(pl.multiple_of(ci * SUBLANES, SUBLANES), SUBLANES), :][0:1, :]
        state = state * egl + lax.dot_general(ke_s[rs, :], vb, (((0,), (0,)), ((), ())),
                                              preferred_element_type=F32)
        on = o * lax.rsqrt(jnp.mean(o * o, axis=-1, keepdims=True) + NORM_EPS) * nw
        o_ref[rs, :] = (on * _silu(z_ref[rs, :].astype(F32))).astype(o_ref.dtype)
        return state

    lax.fori_loop(0, nc, step, jnp.zeros((DN_HEAD_DIM, DN_HEAD_DIM), F32))


def _deltanet(proj, ab, conv_w, alog_b, dtb_b, norm_w, batch, t):
    n = proj.shape[0]
    hb = DN_HEAD_DIM
    qb, kb_, vb_, zb = OFF_DQ // hb, OFF_DK // hb, OFF_DV // hb, OFF_DZ // hb
    col = lambda base: pl.BlockSpec((t, hb), lambda b, h: (b, base + h))
    cw = lambda base: pl.BlockSpec((CONV_WIDTH, hb), lambda b, h: (0, base + h))
    par = pl.BlockSpec((1, 1, LANES), lambda b, h: (h, 0, 0))
    return pl.pallas_call(
        _dn_kernel,
        grid=(batch, DN_HEADS),
        in_specs=[
            col(qb), col(kb_), col(vb_), col(zb),
            pl.BlockSpec((t, LANES), lambda b, h: (b, 0)),
            cw(0), cw(DN_HEADS), cw(2 * DN_HEADS),
            par, par,
            pl.BlockSpec((1, hb), lambda b, h: (0, 0)),
        ],
        out_specs=pl.BlockSpec((t, hb), lambda b, h: (b, h)),
        out_shape=jax.ShapeDtypeStruct((n, DN_WIDTH), BF16),
        scratch_shapes=[
            pltpu.VMEM((t, hb), F32),
            pltpu.VMEM((t, hb), BF16),
            pltpu.VMEM((t, DN_CHUNK), BF16),
            pltpu.VMEM((t, hb), BF16),
            pltpu.VMEM((t, hb), BF16),
            pltpu.VMEM((t // DN_CHUNK * SUBLANES, LANES), F32),
        ],
        compiler_params=_params(("parallel", "parallel")),
        name="deltanet",
    )(proj, proj, proj, proj, ab, conv_w, conv_w, conv_w, alog_b, dtb_b, norm_w)


def _rope(x, cos_t, sin_t):
    half = ROPE_DIMS // 2
    lane = lax.broadcasted_iota(jnp.int32, x.shape, 1)
    swapped = jnp.where(lane < half, pltpu.roll(x, LANES - half, axis=1), pltpu.roll(x, half, axis=1))
    return x * cos_t + swapped * sin_t


def _moba_kernel(q_ref, k_ref, v_ref, cq_ref, sq_ref, ck_ref, sk_ref, o_ref,
                 kr_s, vt_s, km_s, m_s, l_s, acc_s):
    t = k_ref.shape[0]
    blk = MB_BLOCK
    nb = t // blk
    qi = pl.program_id(2)

    @pl.when(qi == 0)
    def _():
        for j in range(nb):
            sl = slice(j * blk, (j + 1) * blk)
            kr = _rope(k_ref[sl, :].astype(F32), ck_ref[sl, :], sk_ref[sl, :])
            kr_s[sl, :] = kr.astype(BF16)
            km_s[j:j + 1, :] = jnp.mean(kr, axis=0, keepdims=True)
            vt_s[:, sl] = v_ref[sl, :].astype(F32).T.astype(BF16)

    q = _rope(q_ref[...].astype(F32), cq_ref[...], sq_ref[...])
    g_t = lax.dot_general(km_s[...], q, (((1,), (1,)), ((), ())), precision=HIGHEST,
                          preferred_element_type=F32)
    bidx = lax.broadcasted_iota(jnp.int32, g_t.shape, 0)
    rank = jnp.zeros(g_t.shape, F32)
    for m in range(nb):
        gm = g_t[m:m + 1, :]
        beats = jnp.where(gm > g_t, 1.0, jnp.where((gm == g_t) & (bidx > m), 1.0, 0.0))
        rank = rank + jnp.where(m < qi, beats, 0.0)
    sel_t = jnp.where((rank < MB_TOP_K) & (bidx < qi), 1.0, 0.0)

    qs = (q * MB_SCALE).astype(BF16)
    krow = lax.broadcasted_iota(jnp.int32, (blk, blk), 0)
    qcol = lax.broadcasted_iota(jnp.int32, (blk, blk), 1)

    own = pl.ds(pl.multiple_of(qi * blk, blk), blk)
    s = lax.dot_general(kr_s[own, :], qs, (((1,), (1,)), ((), ())), preferred_element_type=F32)
    s = jnp.where(krow <= qcol, s, -jnp.inf)
    m0 = jnp.max(s, axis=0, keepdims=True)
    p = jnp.exp(s - m0)
    m_s[...] = m0
    l_s[...] = jnp.sum(p, axis=0, keepdims=True)
    acc_s[...] = jnp.dot(vt_s[:, own], p.astype(BF16), preferred_element_type=F32)

    for j in range(nb - 1):
        @pl.when(j < qi)
        def _(j=j):
            sl = slice(j * blk, (j + 1) * blk)
            s = lax.dot_general(kr_s[sl, :], qs, (((1,), (1,)), ((), ())), preferred_element_type=F32)
            s = jnp.where(sel_t[j:j + 1, :] > 0.5, s, -jnp.inf)
            m_prev = m_s[...]
            m_new = jnp.maximum(m_prev, jnp.max(s, axis=0, keepdims=True))
            alpha = jnp.exp(m_prev - m_new)
            p = jnp.exp(s - m_new)
            m_s[...] = m_new
            l_s[...] = alpha * l_s[...] + jnp.sum(p, axis=0, keepdims=True)
            acc_s[...] = alpha * acc_s[...] + jnp.dot(vt_s[:, sl], p.astype(BF16),
                                                      preferred_element_type=F32)

    o_ref[...] = (acc_s[...] / l_s[...]).T.astype(o_ref.dtype)


def _moba(proj, cos_t, sin_t, batch, t):
    n = proj.shape[0]
    hb = MB_HEAD_DIM
    nb = t // MB_BLOCK
    qb, kb_, vb_ = OFF_MQ // hb, OFF_MK // hb, OFF_MV // hb
    full = lambda base: pl.BlockSpec((t, hb), lambda b, h, i: (b, base + h))
    return pl.pallas_call(
        _moba_kernel,
        grid=(batch, MB_HEADS, nb),
        in_specs=[
            pl.BlockSpec((MB_BLOCK, hb), lambda b, h, i: (b * nb + i, qb + h)),
            full(kb_), full(vb_),
            pl.BlockSpec((MB_BLOCK, hb), lambda b, h, i: (i, 0)),
            pl.BlockSpec((MB_BLOCK, hb), lambda b, h, i: (i, 0)),
            pl.BlockSpec((t, hb), lambda b, h, i: (0, 0)),
            pl.BlockSpec((t, hb), lambda b, h, i: (0, 0)),
        ],
        out_specs=pl.BlockSpec((MB_BLOCK, hb), lambda b, h, i: (b * nb + i, h)),
        out_shape=jax.ShapeDtypeStruct((n, MB_WIDTH), BF16),
        scratch_shapes=[
            pltpu.VMEM((t, hb), BF16),
            pltpu.VMEM((hb, t), BF16),
            pltpu.VMEM((nb, hb), F32),
            pltpu.VMEM((1, MB_BLOCK), F32),
            pltpu.VMEM((1, MB_BLOCK), F32),
            pltpu.VMEM((hb, MB_BLOCK), F32),
        ],
        compiler_params=_params(("parallel", "parallel", "arbitrary")),
        name="moba",
    )(proj, proj, proj, cos_t, sin_t, cos_t, sin_t)


LRU_ROWS = 256


def _lru_kernel(x_ref, y_ref, cw_ref, cb_ref, wg_ref, bg_ref, lam_ref, o_ref, a_s, u_s):
    t = x_ref.shape[0]
    w = LRU_WIDTH
    rows = min(LRU_ROWS, t)
    sp = _softplus(-lam_ref[...])

    def gates(ci, carry):
        r0 = pl.multiple_of(ci * rows, rows)
        cur = x_ref[pl.ds(r0, rows), :].astype(F32)
        xc = _causal_conv(cur, _prev_rows(x_ref, r0), cw_ref[...]) + cb_ref[...]
        ri = jnp.dot(xc.astype(BF16), wg_ref[...], preferred_element_type=F32) + bg_ref[...]
        r = _sigmoid(ri[:, :w])
        i = _sigmoid(ri[:, w:])
        log_a = -LRU_C * r * sp
        a = jnp.exp(log_a)
        e2 = jnp.exp(2.0 * log_a)
        em1 = jnp.where(e2 == 1.0, 2.0 * log_a, (e2 - 1.0) * (2.0 * log_a) / jnp.log(e2))
        a_s[pl.ds(r0, rows), :] = a
        u_s[pl.ds(r0, rows), :] = jnp.sqrt(-em1) * (i * xc)
        return carry

    lax.fori_loop(0, t // rows, gates, 0)

    sub = lax.broadcasted_iota(jnp.int32, (SUBLANES, w), 0)

    def scan(ti, hprev):
        rs = pl.ds(pl.multiple_of(ti * SUBLANES, SUBLANES), SUBLANES)
        a = a_s[rs, :]
        u = u_s[rs, :]
        for s in (1, 2, 4):
            keep = sub >= s
            u = jnp.where(keep, a * pltpu.roll(u, s, axis=0) + u, u)
            a = jnp.where(keep, a * pltpu.roll(a, s, axis=0), a)
        hs = a * hprev + u
        o_ref[rs, :] = (hs * _gelu_tanh(y_ref[rs, :].astype(F32))).astype(o_ref.dtype)
        return hs[SUBLANES - 1:SUBLANES, :]

    lax.fori_loop(0, t // SUBLANES, scan, jnp.zeros((1, w), F32))


def _rglru(proj, conv_w, conv_b, w_gates, b_gates, lam, batch, t):
    n = proj.shape[0]
    w = LRU_WIDTH
    const = lambda shape: pl.BlockSpec(shape, lambda b: (0, 0))
    return pl.pallas_call(
        _lru_kernel,
        grid=(batch,),
        in_specs=[
            pl.BlockSpec((t, w), lambda b: (b, OFF_LX // w)),
            pl.BlockSpec((t, w), lambda b: (b, OFF_LY // w)),
            const((CONV_WIDTH, w)), const((1, w)), const((w, 2 * w)), const((1, 2 * w)), const((1, w)),
        ],
        out_specs=pl.BlockSpec((t, w), lambda b: (b, 0)),
        out_shape=jax.ShapeDtypeStruct((n, w), BF16),
        scratch_shapes=[pltpu.VMEM((t, w), F32), pltpu.VMEM((t, w), F32)],
        compiler_params=_params(("parallel",)),
        name="rglru",
    )(proj, proj, conv_w, conv_b, w_gates, b_gates, lam)


def _merge_kernel(x_ref, g0_ref, g1_ref, g2_ref, bg_ref, dn_ref, mb_ref, lr_ref,
                  wdn_ref, wmb_ref, wlr_ref, wo_ref, o_ref):
    d = D_MODEL
    bg = bg_ref[...]
    mixed = _sigmoid(g0_ref[...].astype(F32) + bg[:, 0:d]) * jnp.dot(
        dn_ref[...], wdn_ref[...], preferred_element_type=F32)
    mixed = mixed + _sigmoid(g1_ref[...].astype(F32) + bg[:, d:2 * d]) * jnp.dot(
        mb_ref[...], wmb_ref[...], preferred_element_type=F32)
    mixed = mixed + _sigmoid(g2_ref[...].astype(F32) + bg[:, 2 * d:3 * d]) * jnp.dot(
        lr_ref[...], wlr_ref[...], preferred_element_type=F32)
    o_ref[...] = x_ref[...] + jnp.dot(mixed.astype(BF16), wo_ref[...], preferred_element_type=F32)


def _merge(x2, proj, b_gate, y_dn, y_mb, y_lru, w_dn, w_mb, w_lru, w_o, tm):
    n, d = x2.shape
    rowblk = lambda width, j: pl.BlockSpec((tm, width), lambda i: (i, j))
    const = lambda shape: pl.BlockSpec(shape, lambda i: (0, 0))
    g0 = OFF_GATES // d
    return pl.pallas_call(
        _merge_kernel,
        grid=(n // tm,),
        in_specs=[
            rowblk(d, 0), rowblk(d, g0), rowblk(d, g0 + 1), rowblk(d, g0 + 2), const((1, N_BRANCH * d)),
            rowblk(DN_WIDTH, 0), rowblk(MB_WIDTH, 0), rowblk(LRU_WIDTH, 0),
            const((DN_WIDTH, d)), const((MB_WIDTH, d)), const((LRU_WIDTH, d)), const((d, d)),
        ],
        out_specs=rowblk(d, 0),
        out_shape=jax.ShapeDtypeStruct((n, d), F32),
        compiler_params=_params(("parallel",)),
        name="merge",
    )(x2, proj, proj, proj, b_gate, y_dn, y_mb, y_lru, w_dn, w_mb, w_lru, w_o)


FFN_CHUNK = 256


def _ffn_kernel(x_ref, g_ref, win_ref, wdn_ref, gf_ref, o_ref, act_s, *, final_norm):
    x = x_ref[...]
    ms = jnp.mean(x * x, axis=-1, keepdims=True)
    hn = (x * lax.rsqrt(ms + NORM_EPS) * g_ref[...]).astype(BF16)
    fc = FFN_CHUNK
    for ci in range(FFN_HIDDEN // fc):
        gu = jnp.dot(hn, win_ref[:, 2 * ci * fc:2 * (ci + 1) * fc], preferred_element_type=F32)
        act_s[:, ci * fc:(ci + 1) * fc] = (_silu(gu[:, :fc]) * gu[:, fc:]).astype(BF16)
    y = x + jnp.dot(act_s[...], wdn_ref[...], preferred_element_type=F32)
    if final_norm:
        ms = jnp.mean(y * y, axis=-1, keepdims=True)
        y = y * lax.rsqrt(ms + NORM_EPS) * gf_ref[...]
    o_ref[...] = y


def _ffn(x2, gain, w_in, w_down, gain_final, tm, final_norm):
    n, d = x2.shape
    const = lambda shape: pl.BlockSpec(shape, lambda i: (0, 0))
    return pl.pallas_call(
        functools.partial(_ffn_kernel, final_norm=final_norm),
        grid=(n // tm,),
        in_specs=[
            pl.BlockSpec((tm, d), lambda i: (i, 0)),
            const((1, d)), const((d, 2 * FFN_HIDDEN)), const((FFN_HIDDEN, d)), const((1, d)),
        ],
        out_specs=pl.BlockSpec((tm, d), lambda i: (i, 0)),
        out_shape=jax.ShapeDtypeStruct((n, d), F32),
        scratch_shapes=[pltpu.VMEM((tm, FFN_HIDDEN), BF16)],
        compiler_params=_params(("parallel",)),
        name="ffn",
    )(x2, gain, w_in, w_down, gain_final)


def _rope_tables(t):
    half = ROPE_DIMS // 2
    inv_freq = ROPE_THETA ** (-jnp.arange(half, dtype=F32) / half)
    ang = jnp.arange(t, dtype=F32)[:, None] * inv_freq[None, :]
    cos, sin = jnp.cos(ang), jnp.sin(ang)
    pad = MB_HEAD_DIM - ROPE_DIMS
    cos_t = jnp.concatenate([cos, cos, jnp.ones((t, pad), F32)], axis=1)
    sin_t = jnp.concatenate([-sin, sin, jnp.zeros((t, pad), F32)], axis=1)
    return cos_t, sin_t


def _block_diag(w):
    g, di, do = w.shape
    eye = jnp.eye(g, dtype=w.dtype)
    return (w[:, :, None, :] * eye[:, None, :, None]).reshape(g * di, g * do)


def _pick_tile(n, pref):
    tile = min(pref, n)
    while n % tile:
        tile //= 2
    return tile


def kernel(x, norm_mix, w_in, b_gate, dn_conv, dn_a_log, dn_dt_bias, dn_norm, dn_out, mb_out, lru_conv_w, lru_conv_b, lru_w_r, lru_b_r, lru_w_i, lru_b_i, lru_lambda, lru_out, w_o, norm_ffn, ffn_in, ffn_down, norm_final):
    batch, t, d = x.shape
    assert d == D_MODEL and t % MB_BLOCK == 0 and t % DN_CHUNK == 0
    n = batch * t
    depth = w_in.shape[0]
    x2 = x.reshape(n, d)
    cos_t, sin_t = _rope_tables(t)
    tm_proj = _pick_tile(n, 1024)
    tm_mix = _pick_tile(n, 512)

    c_ab = 4 * DN_WIDTH
    c_mq = c_ab + 2 * DN_HEADS
    c_g = c_mq + 3 * MB_WIDTH + 2 * LRU_WIDTH

    for l in range(depth):
        wl = w_in[l]
        w_main = jnp.concatenate([wl[:, c_g:], wl[:, :c_ab], wl[:, c_mq:c_g]], axis=1).astype(BF16)
        w_ab = jnp.pad(wl[:, c_ab:c_mq], ((0, 0), (0, LANES - 2 * DN_HEADS))).astype(BF16)
        proj, ab = _norm_proj(x2, norm_mix[l][None, :], w_main, w_ab, tm_proj, 512)

        alog_b = jnp.broadcast_to(dn_a_log[l][:, None, None], (DN_HEADS, 1, LANES))
        dtb_b = jnp.broadcast_to(dn_dt_bias[l][:, None, None], (DN_HEADS, 1, LANES))
        y_dn = _deltanet(proj, ab, dn_conv[l], alog_b, dtb_b, dn_norm[l][None, :], batch, t)

        y_mb = _moba(proj, cos_t, sin_t, batch, t)

        w_gates = jnp.concatenate([_block_diag(lru_w_r[l]), _block_diag(lru_w_i[l])], axis=1).astype(BF16)
        b_gates = jnp.concatenate([lru_b_r[l], lru_b_i[l]])[None, :]
        y_lru = _rglru(proj, lru_conv_w[l], lru_conv_b[l][None, :], w_gates, b_gates,
                       lru_lambda[l][None, :], batch, t)

        x2 = _merge(x2, proj, b_gate[l][None, :], y_dn, y_mb, y_lru,
                    dn_out[l].astype(BF16), mb_out[l].astype(BF16), lru_out[l].astype(BF16),
                    w_o[l].astype(BF16), tm_mix)

        fc = FFN_CHUNK
        wg = ffn_in[l][:, :FFN_HIDDEN].reshape(d, FFN_HIDDEN // fc, fc)
        wu = ffn_in[l][:, FFN_HIDDEN:].reshape(d, FFN_HIDDEN // fc, fc)
        w_ffn_in = jnp.concatenate([wg, wu], axis=2).reshape(d, 2 * FFN_HIDDEN).astype(BF16)
        x2 = _ffn(x2, norm_ffn[l][None, :], w_ffn_in, ffn_down[l].astype(BF16), norm_final[None, :],
                  tm_mix, final_norm=(l == depth - 1))

    return x2.reshape(batch, t, d)
```

```python
import functools

import jax
import jax.numpy as jnp
from jax import lax
from jax.experimental import pallas as pl
from jax.experimental.pallas import tpu as pltpu

F32 = jnp.float32
BF16 = jnp.bfloat16
HIGHEST = lax.Precision.HIGHEST

D_MODEL = 1024
NORM_EPS = 1e-6
CONV_WIDTH = 4
DN_HEADS = 4
DN_HEAD_DIM = 128
DN_WIDTH = DN_HEADS * DN_HEAD_DIM
MB_HEADS = 4
MB_HEAD_DIM = 128
MB_WIDTH = MB_HEADS * MB_HEAD_DIM
MB_BLOCK = 256
MB_TOP_K = 3
MB_SCALE = MB_HEAD_DIM ** -0.5
ROPE_THETA = 500000.0
ROPE_DIMS = MB_HEAD_DIM // 4
LRU_WIDTH = 512
LRU_GROUPS = 8
LRU_GROUP_DIM = LRU_WIDTH // LRU_GROUPS
LRU_C = 8.0
N_BRANCH = 3
FFN_HIDDEN = -(-8 * D_MODEL // (3 * 256)) * 256

LANES = 128
SUBLANES = 8
BF16_ROWS = 16
VMEM_LIMIT = 56 * 1024 * 1024

OFF_GATES = 0
OFF_DQ = OFF_GATES + N_BRANCH * D_MODEL
OFF_DK = OFF_DQ + DN_WIDTH
OFF_DV = OFF_DK + DN_WIDTH
OFF_DZ = OFF_DV + DN_WIDTH
OFF_MQ = OFF_DZ + DN_WIDTH
OFF_MK = OFF_MQ + MB_WIDTH
OFF_MV = OFF_MK + MB_WIDTH
OFF_LX = OFF_MV + MB_WIDTH
OFF_LY = OFF_LX + LRU_WIDTH
PROJ_COLS = OFF_LY + LRU_WIDTH
PROJ_TN = 1536

DN_CHUNK = 128
DN_INV_BLOCK = 16
DN_GROUP = 8


def _params(sem):
    return pltpu.CompilerParams(dimension_semantics=sem, vmem_limit_bytes=VMEM_LIMIT)


def _sigmoid(x):
    return 1.0 / (1.0 + jnp.exp(-x))


def _silu(x):
    return x * _sigmoid(x)


def _softplus(x):
    return jnp.maximum(x, 0.0) + jnp.log1p(jnp.exp(-jnp.abs(x)))


def _gelu_tanh(x):
    c = (2.0 / jnp.pi) ** 0.5
    return 0.5 * x * (1.0 + jnp.tanh(c * (x + 0.044715 * (x * x * x))))


def _bdot(a, b):
    return jnp.dot(a.astype(BF16), b.astype(BF16), preferred_element_type=F32)


def _bdot_nt(a, b):
    return lax.dot_general(a.astype(BF16), b.astype(BF16), (((1,), (1,)), ((), ())),
                           preferred_element_type=F32)


def _bdot_tn(a, b):
    return lax.dot_general(a.astype(BF16), b.astype(BF16), (((0,), (0,)), ((), ())),
                           preferred_element_type=F32)


def _causal_conv(cur, prev, w):
    ext = jnp.concatenate([prev, cur], axis=0)
    out = cur * w[CONV_WIDTH - 1:CONV_WIDTH, :]
    for s in range(1, CONV_WIDTH):
        shifted = pltpu.roll(ext, s, axis=0)[SUBLANES:, :]
        out = out + shifted * w[CONV_WIDTH - 1 - s:CONV_WIDTH - s, :]
    return out


def _prev_rows(ref, r0):
    start = pl.multiple_of(jnp.maximum(r0 - BF16_ROWS, 0), BF16_ROWS)
    rows = ref[pl.ds(start, BF16_ROWS), :].astype(F32)[BF16_ROWS - SUBLANES:, :]
    return jnp.where(r0 > 0, rows, 0.0)


def _norm_proj_kernel(x_ref, g_ref, w_ref, wab_ref, o_ref, oab_ref, hn_ref):
    @pl.when(pl.program_id(1) == 0)
    def _():
        x = x_ref[...]
        ms = jnp.mean(x * x, axis=-1, keepdims=True)
        hn = (x * lax.rsqrt(ms + NORM_EPS) * g_ref[...]).astype(BF16)
        hn_ref[...] = hn
        oab_ref[...] = jnp.dot(hn, wab_ref[...], preferred_element_type=F32)

    o_ref[...] = jnp.dot(hn_ref[...], w_ref[...], preferred_element_type=F32).astype(o_ref.dtype)


def _norm_proj(x2, gain, w_main, w_ab, tm, tn):
    n, d = x2.shape
    cols = w_main.shape[1]
    return pl.pallas_call(
        _norm_proj_kernel,
        grid=(n // tm, cols // tn),
        in_specs=[
            pl.BlockSpec((tm, d), lambda i, j: (i, 0)),
            pl.BlockSpec((1, d), lambda i, j: (0, 0)),
            pl.BlockSpec((d, tn), lambda i, j: (0, j)),
            pl.BlockSpec((d, LANES), lambda i, j: (0, 0)),
        ],
        out_specs=[
            pl.BlockSpec((tm, tn), lambda i, j: (i, j)),
            pl.BlockSpec((tm, LANES), lambda i, j: (i, 0)),
        ],
        out_shape=[
            jax.ShapeDtypeStruct((n, cols), BF16),
            jax.ShapeDtypeStruct((n, LANES), F32),
        ],
        scratch_shapes=[pltpu.VMEM((tm, d), BF16)],
        compiler_params=_params(("parallel", "arbitrary")),
        name="norm_proj",
    )(x2, gain, w_main, w_ab)


def _unit_lower_inverse(lows, row, col):
    c = lows[0].shape[0]
    same_blk = (row // DN_INV_BLOCK) == (col // DN_INV_BLOCK)
    eye = jnp.where(row == col, 1.0, 0.0).astype(F32)
    dgs = [jnp.where(same_blk, low, 0.0) for low in lows]
    offs = [low - dg for low, dg in zip(lows, dgs)]
    inv_ds = [eye - dg for dg in dgs]
    pws = dgs
    for _ in range(DN_INV_BLOCK.bit_length() - 2):
        pws = [_bdot(pw, pw) for pw in pws]
        inv_ds = [inv_d + _bdot(inv_d, pw) for inv_d, pw in zip(inv_ds, pws)]
    pws = [_bdot(inv_d, off) for inv_d, off in zip(inv_ds, offs)]
    inv_ms = [eye - m for m in pws]
    for _ in range((c // DN_INV_BLOCK).bit_length() - 2):
        pws = [_bdot(pw, pw) for pw in pws]
        inv_ms = [inv_m + _bdot(inv_m, pw) for inv_m, pw in zip(inv_ms, pws)]
    return [_bdot(inv_m, inv_d) for inv_m, inv_d in zip(inv_ms, inv_ds)]


def _dn_kernel(q_ref, k_ref, v_ref, z_ref, ab_ref, cwq_ref, cwk_ref, cwv_ref, alog_ref, dtb_ref, nw_ref,
               o_ref, sm_s, sb_s, qe_s, o0_s, st_s, egl_s):
    t = q_ref.shape[0]
    c = DN_CHUNK
    hd = DN_HEAD_DIM
    nc = t // c
    grp = min(DN_GROUP, nc)
    rows_g = grp * c
    h = pl.program_id(1)

    row = lax.broadcasted_iota(jnp.int32, (c, c), 0)
    col = lax.broadcasted_iota(jnp.int32, (c, c), 1)
    tril = jnp.where(row >= col, 1.0, 0.0).astype(F32)
    sel_rows = lax.broadcasted_iota(jnp.int32, (LANES, LANES), 0)
    pick_a = jnp.where(sel_rows == h, 1.0, 0.0).astype(F32)
    pick_b = jnp.where(sel_rows == h + DN_HEADS, 1.0, 0.0).astype(F32)
    neg_a = -jnp.exp(alog_ref[0])
    dtb = dtb_ref[0]

    def conv_silu(ref, cw_ref, r0):
        cur = ref[pl.ds(r0, rows_g), :].astype(F32)
        return _silu(_causal_conv(cur, _prev_rows(ref, r0), cw_ref[...]))

    def l2n(x):
        return x * lax.rsqrt(jnp.sum(x * x, axis=-1, keepdims=True) + NORM_EPS)

    def prep(gi, carry):
        r0 = pl.multiple_of(gi * rows_g, rows_g)
        qg = l2n(conv_silu(q_ref, cwq_ref, r0)) * (DN_HEAD_DIM ** -0.5)
        kg = l2n(conv_silu(k_ref, cwk_ref, r0))
        vg = conv_silu(v_ref, cwv_ref, r0)
        ab = ab_ref[pl.ds(r0, rows_g), :]
        a_b = jnp.dot(ab, pick_a, precision=HIGHEST, preferred_element_type=F32)
        b_b = jnp.dot(ab, pick_b, precision=HIGHEST, preferred_element_type=F32)
        g_b = neg_a * _softplus(a_b + dtb)
        beta_b = _sigmoid(b_b)
        rng = range(grp)
        sls = [slice(ci * c, (ci + 1) * c) for ci in rng]
        qcs, kcs, vcs = [qg[s] for s in sls], [kg[s] for s in sls], [vg[s] for s in sls]
        betas = [beta_b[s] for s in sls]
        gcbs = [jnp.dot(tril, g_b[s], precision=HIGHEST, preferred_element_type=F32) for s in sls]
        gcrs = [gcb.T for gcb in gcbs]
        decays = [jnp.exp(jnp.where(row >= col, gcb - gcr, -jnp.inf)) for gcb, gcr in zip(gcbs, gcrs)]
        kbs = [kc * beta for kc, beta in zip(kcs, betas)]
        lows = [jnp.where(row > col, _bdot_nt(kb, kc) * decay, 0.0) for kb, kc, decay in zip(kbs, kcs, decays)]
        t_invs = _unit_lower_inverse(lows, row, col)
        egs = [jnp.exp(gcb) for gcb in gcbs]
        us = [_bdot(t_inv, vc * beta) for t_inv, vc, beta in zip(t_invs, vcs, betas)]
        ws = [_bdot(t_inv, kb * eg) for t_inv, kb, eg in zip(t_invs, kbs, egs)]
        attns = [_bdot_nt(qc, kc) * decay for qc, kc, decay in zip(qcs, kcs, decays)]
        gls = [gcb[c - 1:c, :] for gcb in gcbs]
        kes = [kc * jnp.exp(gl - gcb) for kc, gl, gcb in zip(kcs, gls, gcbs)]
        wus = [jnp.concatenate([w, u], axis=1) for w, u in zip(ws, us)]
        kws = [_bdot_tn(ke, wu) for ke, wu in zip(kes, wus)]
        aws = [_bdot(attn, wu) for attn, wu in zip(attns, wus)]
        for ci in rng:
            rs = pl.ds(pl.multiple_of(r0 + ci * c, c), c)
            sm_s[rs, :] = kws[ci][:, :hd].astype(BF16)
            sb_s[rs, :] = kws[ci][:, hd:]
            qe_s[rs, :] = (qcs[ci] * egs[ci] - aws[ci][:, :hd]).astype(BF16)
            o0_s[rs, :] = aws[ci][:, hd:]
            es = pl.ds(pl.multiple_of((gi * grp + ci) * SUBLANES, SUBLANES), SUBLANES)
            egl_s[es, :] = jnp.broadcast_to(jnp.exp(gls[ci]), (SUBLANES, LANES))
        return carry

    lax.fori_loop(0, nc // grp, prep, 0)

    def step(ci, state):
        rs = pl.ds(pl.multiple_of(ci * c, c), c)
        sb = state.astype(BF16)
        st_s[rs, :] = sb
        egl = egl_s[pl.ds(pl.multiple_of(ci * SUBLANES, SUBLANES), SUBLANES), :][0:1, :]
        return state * egl - jnp.dot(sm_s[rs, :], sb, preferred_element_type=F32) + sb_s[rs, :]

    lax.fori_loop(0, nc, step, jnp.zeros((hd, hd), F32))

    nw = nw_ref[...]

    def post(gi, carry):
        r0 = pl.multiple_of(gi * rows_g, rows_g)
        rss = [pl.ds(pl.multiple_of(r0 + ci * c, c), c) for ci in range(grp)]
        os_ = [jnp.dot(qe_s[rs, :], st_s[rs, :], preferred_element_type=F32) + o0_s[rs, :] for rs in rss]
        for rs, o in zip(rss, os_):
            on = o * lax.rsqrt(jnp.mean(o * o, axis=-1, keepdims=True) + NORM_EPS) * nw
            o_ref[rs, :] = (on * _silu(z_ref[rs, :].astype(F32))).astype(o_ref.dtype)
        return carry

    lax.fori_loop(0, nc // grp, post, 0)


def _deltanet(proj, ab, conv_w, alog_b, dtb_b, norm_w, batch, t):
    n = proj.shape[0]
    hb = DN_HEAD_DIM
    qb, kb_, vb_, zb = OFF_DQ // hb, OFF_DK // hb, OFF_DV // hb, OFF_DZ // hb
    col = lambda base: pl.BlockSpec((t, hb), lambda b, h: (b, base + h))
    cw = lambda base: pl.BlockSpec((CONV_WIDTH, hb), lambda b, h: (0, base + h))
    par = pl.BlockSpec((1, 1, LANES), lambda b, h: (h, 0, 0))
    return pl.pallas_call(
        _dn_kernel,
        grid=(batch, DN_HEADS),
        in_specs=[
            col(qb), col(kb_), col(vb_), col(zb),
            pl.BlockSpec((t, LANES), lambda b, h: (b, 0)),
            cw(0), cw(DN_HEADS), cw(2 * DN_HEADS),
            par, par,
            pl.BlockSpec((1, hb), lambda b, h: (0, 0)),
        ],
        out_specs=pl.BlockSpec((t, hb), lambda b, h: (b, h)),
        out_shape=jax.ShapeDtypeStruct((n, DN_WIDTH), BF16),
        scratch_shapes=[
            pltpu.VMEM((t, hb), BF16),
            pltpu.VMEM((t, hb), F32),
            pltpu.VMEM((t, hb), BF16),
            pltpu.VMEM((t, hb), F32),
            pltpu.VMEM((t, hb), BF16),
            pltpu.VMEM((t // DN_CHUNK * SUBLANES, LANES), F32),
        ],
        compiler_params=_params(("parallel", "parallel")),
        name="deltanet",
    )(proj, proj, proj, proj, ab, conv_w, conv_w, conv_w, alog_b, dtb_b, norm_w)


def _rope(x, cos_t, sin_t):
    half = ROPE_DIMS // 2
    lane = lax.broadcasted_iota(jnp.int32, x.shape, 1)
    swapped = jnp.where(lane < half, pltpu.roll(x, LANES - half, axis=1), pltpu.roll(x, half, axis=1))
    return x * cos_t + swapped * sin_t


MB_Q_GROUP = 2


def _moba_kernel(q_ref, k_ref, v_ref, cos_ref, sin_ref, o_ref, kr_s, vt_s, qs_s):
    t = k_ref.shape[0]
    blk = MB_BLOCK
    nb = t // blk
    blocks = [slice(j * blk, (j + 1) * blk) for j in range(nb)]

    kms = []
    for sl in blocks:
        kr = _rope(k_ref[sl, :].astype(F32), cos_ref[sl, :], sin_ref[sl, :])
        kr_s[sl, :] = kr.astype(BF16)
        kms.append(jnp.mean(kr, axis=0, keepdims=True))
        vt_s[:, sl] = v_ref[sl, :].astype(F32).T.astype(BF16)
    km = jnp.concatenate(kms, axis=0)

    bidx = lax.broadcasted_iota(jnp.int32, (nb, blk), 0)
    sels = []
    for i, sl in enumerate(blocks):
        q = _rope(q_ref[sl, :].astype(F32), cos_ref[sl, :], sin_ref[sl, :])
        qs_s[sl, :] = (q * MB_SCALE).astype(BF16)
        g_t = lax.dot_general(km, q, (((1,), (1,)), ((), ())), precision=HIGHEST,
                              preferred_element_type=F32)
        rank = jnp.zeros((nb, blk), F32)
        for m in range(i):
            gm = g_t[m:m + 1, :]
            rank = rank + jnp.where(gm > g_t, 1.0, jnp.where((gm == g_t) & (bidx > m), 1.0, 0.0))
        sels.append(jnp.where((rank < MB_TOP_K) & (bidx < i), 1.0, 0.0))

    krow = lax.broadcasted_iota(jnp.int32, (blk, blk), 0)
    qcol = lax.broadcasted_iota(jnp.int32, (blk, blk), 1)

    for g0 in range(0, nb, MB_Q_GROUP):
        tiles = list(range(g0, min(g0 + MB_Q_GROUP, nb)))
        ss = {}
        for i in tiles:
            for j in range(i + 1):
                s = lax.dot_general(kr_s[blocks[j], :], qs_s[blocks[i], :], (((1,), (1,)), ((), ())),
                                    preferred_element_type=F32)
                mask = (krow <= qcol) if j == i else (sels[i][j:j + 1, :] > 0.5)
                ss[i, j] = jnp.where(mask, s, -jnp.inf)
        m_all = {i: functools.reduce(jnp.maximum, [jnp.max(ss[i, j], axis=0, keepdims=True)
                                                   for j in range(i + 1)]) for i in tiles}
        ps = {(i, j): jnp.exp(ss[i, j] - m_all[i]) for (i, j) in ss}
        for i in tiles:
            l_all = functools.reduce(jnp.add, [jnp.sum(ps[i, j], axis=0, keepdims=True) for j in range(i + 1)])
            p_all = jnp.concatenate([ps[i, j].astype(BF16) for j in range(i + 1)], axis=0)
            acc = jnp.dot(vt_s[:, :(i + 1) * blk], p_all, preferred_element_type=F32)
            o_ref[blocks[i], :] = (acc / l_all).T.astype(o_ref.dtype)


def _moba(proj, cos_t, sin_t, batch, t):
    n = proj.shape[0]
    hb = MB_HEAD_DIM
    qb, kb_, vb_ = OFF_MQ // hb, OFF_MK // hb, OFF_MV // hb
    full = lambda base: pl.BlockSpec((t, hb), lambda b, h: (b, base + h))
    table = pl.BlockSpec((t, hb), lambda b, h: (0, 0))
    return pl.pallas_call(
        _moba_kernel,
        grid=(batch, MB_HEADS),
        in_specs=[full(qb), full(kb_), full(vb_), table, table],
        out_specs=pl.BlockSpec((t, hb), lambda b, h: (b, h)),
        out_shape=jax.ShapeDtypeStruct((n, MB_WIDTH), BF16),
        scratch_shapes=[
            pltpu.VMEM((t, hb), BF16),
            pltpu.VMEM((hb, t), BF16),
            pltpu.VMEM((t, hb), BF16),
        ],
        compiler_params=_params(("parallel", "parallel")),
        name="moba",
    )(proj, proj, proj, cos_t, sin_t)


LRU_ROWS = 256


def _lru_kernel(x_ref, y_ref, cw_ref, cb_ref, wg_ref, bg_ref, lam_ref, o_ref, a_s, u_s):
    t = x_ref.shape[0]
    w = LRU_WIDTH
    rows = min(LRU_ROWS, t)
    sp = _softplus(-lam_ref[...])

    def gates(ci, carry):
        r0 = pl.multiple_of(ci * rows, rows)
        cur = x_ref[pl.ds(r0, rows), :].astype(F32)
        xc = _causal_conv(cur, _prev_rows(x_ref, r0), cw_ref[...]) + cb_ref[...]
        ri = jnp.dot(xc.astype(BF16), wg_ref[...], preferred_element_type=F32) + bg_ref[...]
        r = _sigmoid(ri[:, :w])
        i = _sigmoid(ri[:, w:])
        log_a = -LRU_C * r * sp
        a = jnp.exp(log_a)
        e2 = jnp.exp(2.0 * log_a)
        em1 = jnp.where(e2 == 1.0, 2.0 * log_a, (e2 - 1.0) * (2.0 * log_a) / jnp.log(e2))
        a_s[pl.ds(r0, rows), :] = a
        u_s[pl.ds(r0, rows), :] = jnp.sqrt(-em1) * (i * xc)
        return carry

    lax.fori_loop(0, t // rows, gates, 0)

    sub = lax.broadcasted_iota(jnp.int32, (SUBLANES, w), 0)

    def scan(ti, hprev):
        rs = pl.ds(pl.multiple_of(ti * SUBLANES, SUBLANES), SUBLANES)
        a = a_s[rs, :]
        u = u_s[rs, :]
        for s in (1, 2, 4):
            keep = sub >= s
            u = jnp.where(keep, a * pltpu.roll(u, s, axis=0) + u, u)
            a = jnp.where(keep, a * pltpu.roll(a, s, axis=0), a)
        hs = a * hprev + u
        o_ref[rs, :] = (hs * _gelu_tanh(y_ref[rs, :].astype(F32))).astype(o_ref.dtype)
        return hs[SUBLANES - 1:SUBLANES, :]

    lax.fori_loop(0, t // SUBLANES, scan, jnp.zeros((1, w), F32))


def _rglru(proj, conv_w, conv_b, w_gates, b_gates, lam, batch, t):
    n = proj.shape[0]
    w = LRU_WIDTH
    const = lambda shape: pl.BlockSpec(shape, lambda b: (0, 0))
    return pl.pallas_call(
        _lru_kernel,
        grid=(batch,),
        in_specs=[
            pl.BlockSpec((t, w), lambda b: (b, OFF_LX // w)),
            pl.BlockSpec((t, w), lambda b: (b, OFF_LY // w)),
            const((CONV_WIDTH, w)), const((1, w)), const((w, 2 * w)), const((1, 2 * w)), const((1, w)),
        ],
        out_specs=pl.BlockSpec((t, w), lambda b: (b, 0)),
        out_shape=jax.ShapeDtypeStruct((n, w), BF16),
        scratch_shapes=[pltpu.VMEM((t, w), F32), pltpu.VMEM((t, w), F32)],
        compiler_params=_params(("parallel",)),
        name="rglru",
    )(proj, proj, conv_w, conv_b, w_gates, b_gates, lam)


def _merge_kernel(x_ref, g0_ref, g1_ref, g2_ref, bg_ref, dn_ref, mb_ref, lr_ref,
                  wdn_ref, wmb_ref, wlr_ref, wo_ref, o_ref):
    d = D_MODEL
    bg = bg_ref[...]
    mixed = _sigmoid(g0_ref[...].astype(F32) + bg[:, 0:d]) * jnp.dot(
        dn_ref[...], wdn_ref[...], preferred_element_type=F32)
    mixed = mixed + _sigmoid(g1_ref[...].astype(F32) + bg[:, d:2 * d]) * jnp.dot(
        mb_ref[...], wmb_ref[...], preferred_element_type=F32)
    mixed = mixed + _sigmoid(g2_ref[...].astype(F32) + bg[:, 2 * d:3 * d]) * jnp.dot(
        lr_ref[...], wlr_ref[...], preferred_element_type=F32)
    o_ref[...] = x_ref[...] + jnp.dot(mixed.astype(BF16), wo_ref[...], preferred_element_type=F32)


def _merge(x2, proj, b_gate, y_dn, y_mb, y_lru, w_dn, w_mb, w_lru, w_o, tm):
    n, d = x2.shape
    rowblk = lambda width, j: pl.BlockSpec((tm, width), lambda i: (i, j))
    const = lambda shape: pl.BlockSpec(shape, lambda i: (0, 0))
    g0 = OFF_GATES // d
    return pl.pallas_call(
        _merge_kernel,
        grid=(n // tm,),
        in_specs=[
            rowblk(d, 0), rowblk(d, g0), rowblk(d, g0 + 1), rowblk(d, g0 + 2), const((1, N_BRANCH * d)),
            rowblk(DN_WIDTH, 0), rowblk(MB_WIDTH, 0), rowblk(LRU_WIDTH, 0),
            const((DN_WIDTH, d)), const((MB_WIDTH, d)), const((LRU_WIDTH, d)), const((d, d)),
        ],
        out_specs=rowblk(d, 0),
        out_shape=jax.ShapeDtypeStruct((n, d), F32),
        compiler_params=_params(("parallel",)),
        name="merge",
    )(x2, proj, proj, proj, b_gate, y_dn, y_mb, y_lru, w_dn, w_mb, w_lru, w_o)


FFN_CHUNK = 256


def _ffn_kernel(x_ref, g_ref, win_ref, wdn_ref, gf_ref, o_ref, act_s, *, final_norm):
    x = x_ref[...]
    ms = jnp.mean(x * x, axis=-1, keepdims=True)
    hn = (x * lax.rsqrt(ms + NORM_EPS) * g_ref[...]).astype(BF16)
    fc = FFN_CHUNK
    for ci in range(FFN_HIDDEN // fc):
        gu = jnp.dot(hn, win_ref[:, 2 * ci * fc:2 * (ci + 1) * fc], preferred_element_type=F32)
        act_s[:, ci * fc:(ci + 1) * fc] = (_silu(gu[:, :fc]) * gu[:, fc:]).astype(BF16)
    y = x + jnp.dot(act_s[...], wdn_ref[...], preferred_element_type=F32)
    if final_norm:
        ms = jnp.mean(y * y, axis=-1, keepdims=True)
        y = y * lax.rsqrt(ms + NORM_EPS) * gf_ref[...]
    o_ref[...] = y


def _ffn(x2, gain, w_in, w_down, gain_final, tm, final_norm):
    n, d = x2.shape
    const = lambda shape: pl.BlockSpec(shape, lambda i: (0, 0))
    return pl.pallas_call(
        functools.partial(_ffn_kernel, final_norm=final_norm),
        grid=(n // tm,),
        in_specs=[
            pl.BlockSpec((tm, d), lambda i: (i, 0)),
            const((1, d)), const((d, 2 * FFN_HIDDEN)), const((FFN_HIDDEN, d)), const((1, d)),
        ],
        out_specs=pl.BlockSpec((tm, d), lambda i: (i, 0)),
        out_shape=jax.ShapeDtypeStruct((n, d), F32),
        scratch_shapes=[pltpu.VMEM((tm, FFN_HIDDEN), BF16)],
        compiler_params=_params(("parallel",)),
        name="ffn",
    )(x2, gain, w_in, w_down, gain_final)


def _rope_tables(t):
    half = ROPE_DIMS // 2
    inv_freq = ROPE_THETA ** (-jnp.arange(half, dtype=F32) / half)
    ang = jnp.arange(t, dtype=F32)[:, None] * inv_freq[None, :]
    cos, sin = jnp.cos(ang), jnp.sin(ang)
    pad = MB_HEAD_DIM - ROPE_DIMS
    cos_t = jnp.concatenate([cos, cos, jnp.ones((t, pad), F32)], axis=1)
    sin_t = jnp.concatenate([-sin, sin, jnp.zeros((t, pad), F32)], axis=1)
    return cos_t, sin_t


def _block_diag(w):
    g, di, do = w.shape
    eye = jnp.eye(g, dtype=w.dtype)
    return (w[:, :, None, :] * eye[:, None, :, None]).reshape(g * di, g * do)


def _pick_tile(n, pref):
    tile = min(pref, n)
    while n % tile:
        tile //= 2
    return tile


def kernel(x, norm_mix, w_in, b_gate, dn_conv, dn_a_log, dn_dt_bias, dn_norm, dn_out, mb_out, lru_conv_w, lru_conv_b, lru_w_r, lru_b_r, lru_w_i, lru_b_i, lru_lambda, lru_out, w_o, norm_ffn, ffn_in, ffn_down, norm_final):
    batch, t, d = x.shape
    assert d == D_MODEL and t % MB_BLOCK == 0 and t % DN_CHUNK == 0
    n = batch * t
    depth = w_in.shape[0]
    x2 = x.reshape(n, d)
    cos_t, sin_t = _rope_tables(t)
    tm_proj = _pick_tile(n, 1024)
    tm_mix = _pick_tile(n, 512)

    c_ab = 4 * DN_WIDTH
    c_mq = c_ab + 2 * DN_HEADS
    c_g = c_mq + 3 * MB_WIDTH + 2 * LRU_WIDTH

    for l in range(depth):
        wl = w_in[l]
        w_main = jnp.concatenate([wl[:, c_g:], wl[:, :c_ab], wl[:, c_mq:c_g]], axis=1).astype(BF16)
        w_ab = jnp.pad(wl[:, c_ab:c_mq], ((0, 0), (0, LANES - 2 * DN_HEADS))).astype(BF16)
        proj, ab = _norm_proj(x2, norm_mix[l][None, :], w_main, w_ab, tm_proj, PROJ_TN)

        alog_b = jnp.broadcast_to(dn_a_log[l][:, None, None], (DN_HEADS, 1, LANES))
        dtb_b = jnp.broadcast_to(dn_dt_bias[l][:, None, None], (DN_HEADS, 1, LANES))
        y_dn = _deltanet(proj, ab, dn_conv[l], alog_b, dtb_b, dn_norm[l][None, :], batch, t)

        y_mb = _moba(proj, cos_t, sin_t, batch, t)

        w_gates = jnp.concatenate([_block_diag(lru_w_r[l]), _block_diag(lru_w_i[l])], axis=1).astype(BF16)
        b_gates = jnp.concatenate([lru_b_r[l], lru_b_i[l]])[None, :]
        y_lru = _rglru(proj, lru_conv_w[l], lru_conv_b[l][None, :], w_gates, b_gates,
                       lru_lambda[l][None, :], batch, t)

        x2 = _merge(x2, proj, b_gate[l][None, :], y_dn, y_mb, y_lru,
                    dn_out[l].astype(BF16), mb_out[l].astype(BF16), lru_out[l].astype(BF16),
                    w_o[l].astype(BF16), tm_mix)

        fc = FFN_CHUNK
        wg = ffn_in[l][:, :FFN_HIDDEN].reshape(d, FFN_HIDDEN // fc, fc)
        wu = ffn_in[l][:, FFN_HIDDEN:].reshape(d, FFN_HIDDEN // fc, fc)
        w_ffn_in = jnp.concatenate([wg, wu], axis=2).reshape(d, 2 * FFN_HIDDEN).astype(BF16)
        x2 = _ffn(x2, norm_ffn[l][None, :], w_ffn_in, ffn_down[l].astype(BF16), norm_final[None, :],
                  tm_mix, final_norm=(l == depth - 1))

    return x2.reshape(batch, t, d)
```

```python
import functools

import jax
import jax.numpy as jnp
from jax import lax
from jax.experimental import pallas as pl
from jax.experimental.pallas import tpu as pltpu

F32 = jnp.float32
BF16 = jnp.bfloat16
HIGHEST = lax.Precision.HIGHEST

D_MODEL = 1024
NORM_EPS = 1e-6
CONV_WIDTH = 4
DN_HEADS = 4
DN_HEAD_DIM = 128
DN_WIDTH = DN_HEADS * DN_HEAD_DIM
MB_HEADS = 4
MB_HEAD_DIM = 128
MB_WIDTH = MB_HEADS * MB_HEAD_DIM
MB_BLOCK = 256
MB_TOP_K = 3
MB_SCALE = MB_HEAD_DIM ** -0.5
ROPE_THETA = 500000.0
ROPE_DIMS = MB_HEAD_DIM // 4
LRU_WIDTH = 512
LRU_GROUPS = 8
LRU_GROUP_DIM = LRU_WIDTH // LRU_GROUPS
LRU_C = 8.0
N_BRANCH = 3
FFN_HIDDEN = -(-8 * D_MODEL // (3 * 256)) * 256

LANES = 128
SUBLANES = 8
BF16_ROWS = 16
VMEM_LIMIT = 56 * 1024 * 1024

OFF_GATES = 0
OFF_DQ = OFF_GATES + N_BRANCH * D_MODEL
OFF_DK = OFF_DQ + DN_WIDTH
OFF_DV = OFF_DK + DN_WIDTH
OFF_DZ = OFF_DV + DN_WIDTH
OFF_MQ = OFF_DZ + DN_WIDTH
OFF_MK = OFF_MQ + MB_WIDTH
OFF_MV = OFF_MK + MB_WIDTH
OFF_LX = OFF_MV + MB_WIDTH
OFF_LY = OFF_LX + LRU_WIDTH
PROJ_COLS = OFF_LY + LRU_WIDTH
PROJ_TN = 1536

DN_CHUNK = 128
DN_INV_BLOCK = 16
DN_GROUP = 16
DN_POST_GROUP = 8


def _lspec(l, shape, index_map):
    return pl.BlockSpec((None,) + shape, lambda *g: (l,) + index_map(*g))


def _params(sem):
    return pltpu.CompilerParams(dimension_semantics=sem, vmem_limit_bytes=VMEM_LIMIT)


def _sigmoid(x):
    return 1.0 / (1.0 + jnp.exp(-x))


def _silu(x):
    return x * _sigmoid(x)


def _softplus(x):
    return jnp.maximum(x, 0.0) + jnp.log1p(jnp.exp(-jnp.abs(x)))


def _gelu_tanh(x):
    c = (2.0 / jnp.pi) ** 0.5
    return 0.5 * x * (1.0 + jnp.tanh(c * (x + 0.044715 * (x * x * x))))


def _bdot(a, b):
    return jnp.dot(a.astype(BF16), b.astype(BF16), preferred_element_type=F32)


def _bdot_nt(a, b):
    return lax.dot_general(a.astype(BF16), b.astype(BF16), (((1,), (1,)), ((), ())),
                           preferred_element_type=F32)


def _bdot_tn(a, b):
    return lax.dot_general(a.astype(BF16), b.astype(BF16), (((0,), (0,)), ((), ())),
                           preferred_element_type=F32)


def _causal_conv(cur, prev, w):
    ext = jnp.concatenate([prev, cur], axis=0)
    out = cur * w[CONV_WIDTH - 1:CONV_WIDTH, :]
    for s in range(1, CONV_WIDTH):
        shifted = pltpu.roll(ext, s, axis=0)[SUBLANES:, :]
        out = out + shifted * w[CONV_WIDTH - 1 - s:CONV_WIDTH - s, :]
    return out


def _prev_rows(ref, r0):
    start = pl.multiple_of(jnp.maximum(r0 - BF16_ROWS, 0), BF16_ROWS)
    rows = ref[pl.ds(start, BF16_ROWS), :].astype(F32)[BF16_ROWS - SUBLANES:, :]
    return jnp.where(r0 > 0, rows, 0.0)


def _norm_proj_kernel(x_ref, g_ref, w_ref, wab_ref, o_ref, oab_ref, hn_ref):
    @pl.when(pl.program_id(1) == 0)
    def _():
        x = x_ref[...]
        ms = jnp.mean(x * x, axis=-1, keepdims=True)
        hn = (x * lax.rsqrt(ms + NORM_EPS) * g_ref[...]).astype(BF16)
        hn_ref[...] = hn
        oab_ref[...] = jnp.dot(hn, wab_ref[...], preferred_element_type=F32)

    o_ref[...] = jnp.dot(hn_ref[...], w_ref[...], preferred_element_type=F32).astype(o_ref.dtype)


def _norm_proj(x2, gain, w_main, w_ab, l, tm, tn):
    n, d = x2.shape
    cols = w_main.shape[2]
    return pl.pallas_call(
        _norm_proj_kernel,
        grid=(n // tm, cols // tn),
        in_specs=[
            pl.BlockSpec((tm, d), lambda i, j: (i, 0)),
            _lspec(l, (1, d), lambda i, j: (0, 0)),
            _lspec(l, (d, tn), lambda i, j: (0, j)),
            _lspec(l, (d, LANES), lambda i, j: (0, 0)),
        ],
        out_specs=[
            pl.BlockSpec((tm, tn), lambda i, j: (i, j)),
            pl.BlockSpec((tm, LANES), lambda i, j: (i, 0)),
        ],
        out_shape=[
            jax.ShapeDtypeStruct((n, cols), BF16),
            jax.ShapeDtypeStruct((n, LANES), F32),
        ],
        scratch_shapes=[pltpu.VMEM((tm, d), BF16)],
        compiler_params=_params(("parallel", "arbitrary")),
        name="norm_proj",
    )(x2, gain, w_main, w_ab)


def _unit_lower_inverse(lows, row, col):
    c = lows[0].shape[0]
    same_blk = (row // DN_INV_BLOCK) == (col // DN_INV_BLOCK)
    eye = jnp.where(row == col, 1.0, 0.0).astype(F32)

    def neumann(ns, nilpotency):
        invs = [eye - n for n in ns]
        pws = [_bdot(n, n) for n in ns]
        levels = nilpotency.bit_length() - 2
        for lvl in range(levels):
            if lvl < levels - 1:
                both = [_bdot(pw, jnp.concatenate([pw, inv], axis=1)) for pw, inv in zip(pws, invs)]
                pws = [b[:, :c] for b in both]
                invs = [inv + b[:, c:] for inv, b in zip(invs, both)]
            else:
                invs = [inv + _bdot(pw, inv) for pw, inv in zip(pws, invs)]
        return invs

    dgs = [jnp.where(same_blk, low, 0.0) for low in lows]
    offs = [low - dg for low, dg in zip(lows, dgs)]
    inv_ds = neumann(dgs, DN_INV_BLOCK)
    inv_ms = neumann([_bdot(inv_d, off) for inv_d, off in zip(inv_ds, offs)], c // DN_INV_BLOCK)
    return [_bdot(inv_m, inv_d) for inv_m, inv_d in zip(inv_ms, inv_ds)]


def _dn_kernel(q_ref, k_ref, v_ref, z_ref, ab_ref, cwq_ref, cwk_ref, cwv_ref, alog_ref, dtb_ref, nw_ref,
               o_ref, sm_s, sb_s, qe_s, o0_s, st_s, egl_s):
    t = q_ref.shape[0]
    c = DN_CHUNK
    hd = DN_HEAD_DIM
    nc = t // c
    grp = min(DN_GROUP, nc)
    rows_g = grp * c
    h = pl.program_id(1)

    row = lax.broadcasted_iota(jnp.int32, (c, c), 0)
    col = lax.broadcasted_iota(jnp.int32, (c, c), 1)
    r2 = lax.broadcasted_iota(jnp.int32, (c, 2 * c), 0)
    c2 = lax.broadcasted_iota(jnp.int32, (c, 2 * c), 1)
    tril2 = jnp.where(r2 >= c2 % c, 1.0, 0.0).astype(BF16)
    pr = lax.broadcasted_iota(jnp.int32, (2 * LANES, 2 * LANES), 0) % LANES
    pc = lax.broadcasted_iota(jnp.int32, (2 * LANES, 2 * LANES), 1)
    pick2 = jnp.where(pr == jnp.where(pc < LANES, h, h + DN_HEADS), 1.0, 0.0).astype(BF16)
    sr = lax.broadcasted_iota(jnp.int32, ((CONV_WIDTH - 1) * c, 2 * c), 0)
    sc = lax.broadcasted_iota(jnp.int32, ((CONV_WIDTH - 1) * c, 2 * c), 1)
    shift_m = jnp.where(sc == c + sr % c - (sr // c + 1), 1.0, 0.0).astype(BF16)
    neg_a = -jnp.exp(alog_ref[...])
    dtb = dtb_ref[...]
    cw = jnp.concatenate([cwq_ref[...], cwk_ref[...], cwv_ref[...]], axis=1)

    def split2(x, axis):
        hi = x.astype(BF16)
        lo = (x - hi.astype(F32)).astype(BF16)
        return jnp.concatenate([hi, lo], axis=axis)

    def l2n(x):
        return x * lax.rsqrt(jnp.sum(x * x, axis=-1, keepdims=True) + NORM_EPS)

    def prep(gi, carry):
        r0 = pl.multiple_of(gi * rows_g, rows_g)
        rows = pl.ds(r0, rows_g)
        before = pl.ds(pl.multiple_of(jnp.maximum(r0 - c, 0), c), c)
        xg = jnp.concatenate([q_ref[rows, :], k_ref[rows, :], v_ref[rows, :]], axis=1).astype(BF16)
        xp = jnp.concatenate([q_ref[before, :], k_ref[before, :], v_ref[before, :]], axis=1).astype(BF16)
        xp = jnp.where(r0 > 0, xp, jnp.zeros_like(xp))
        ab2 = split2(ab_ref[rows, :], 1)
        agb = jnp.dot(ab2, pick2, preferred_element_type=F32)
        g_b = neg_a * _softplus(agb[:, :LANES] + dtb)
        beta_b = _sigmoid(agb[:, LANES:])
        rng = range(grp)
        sls = [slice(ci * c, (ci + 1) * c) for ci in rng]
        xcs = [xg[s] for s in sls]
        xxs = [jnp.concatenate([xp if ci == 0 else xcs[ci - 1], xcs[ci]], axis=0) for ci in rng]
        shs = [jnp.dot(shift_m, xx, preferred_element_type=F32) for xx in xxs]
        convs = [xc.astype(F32) * cw[CONV_WIDTH - 1:CONV_WIDTH, :]
                 + sh[0:c] * cw[2:3, :] + sh[c:2 * c] * cw[1:2, :] + sh[2 * c:3 * c] * cw[0:1, :]
                 for xc, sh in zip(xcs, shs)]
        acts = [_silu(cv) for cv in convs]
        qcs = [l2n(a[:, :hd]) * (DN_HEAD_DIM ** -0.5) for a in acts]
        kcs = [l2n(a[:, hd:2 * hd]) for a in acts]
        vcs = [a[:, 2 * hd:] for a in acts]
        betas = [beta_b[s] for s in sls]
        gcbs = [jnp.dot(tril2, split2(g_b[s], 0), preferred_element_type=F32) for s in sls]
        gcrs = [gcb.T for gcb in gcbs]
        decays = [jnp.exp(jnp.where(row >= col, gcb - gcr, -jnp.inf)) for gcb, gcr in zip(gcbs, gcrs)]
        kbs = [kc * beta for kc, beta in zip(kcs, betas)]
        lows = [jnp.where(row > col, _bdot_nt(kb, kc) * decay, 0.0) for kb, kc, decay in zip(kbs, kcs, decays)]
        t_invs = _unit_lower_inverse(lows, row, col)
        egs = [jnp.exp(gcb) for gcb in gcbs]
        wus = [_bdot(t_inv, jnp.concatenate([kb * eg, vc * beta], axis=1))
               for t_inv, kb, eg, vc, beta in zip(t_invs, kbs, egs, vcs, betas)]
        attns = [_bdot_nt(qc, kc) * decay for qc, kc, decay in zip(qcs, kcs, decays)]
        gls = [gcb[c - 1:c, :] for gcb in gcbs]
        kes = [kc * jnp.exp(gl - gcb) for kc, gl, gcb in zip(kcs, gls, gcbs)]
        kws = [_bdot_tn(ke, wu) for ke, wu in zip(kes, wus)]
        aws = [_bdot(attn, wu) for attn, wu in zip(attns, wus)]
        for ci in rng:
            rs = pl.ds(pl.multiple_of(r0 + ci * c, c), c)
            sm_s[h, rs, :] = kws[ci][:, :hd].astype(BF16)
            sb_s[h, rs, :] = kws[ci][:, hd:]
            qe_s[h, rs, :] = (qcs[ci] * egs[ci] - aws[ci][:, :hd]).astype(BF16)
            o0_s[h, rs, :] = aws[ci][:, hd:]
            es = pl.ds(pl.multiple_of((gi * grp + ci) * SUBLANES, SUBLANES), SUBLANES)
            egl_s[h, es, :] = jnp.broadcast_to(jnp.exp(gls[ci]), (SUBLANES, LANES))
        return carry

    lax.fori_loop(0, nc // grp, prep, 0)

    @pl.when(h == DN_HEADS - 1)
    def _():
        heads = range(DN_HEADS)

        def step(ci, states):
            rs = pl.ds(pl.multiple_of(ci * c, c), c)
            es = pl.ds(pl.multiple_of(ci * SUBLANES, SUBLANES), SUBLANES)
            sbs = [s.astype(BF16) for s in states]
            for hh in heads:
                st_s[hh, rs, :] = sbs[hh]
            upd = [jnp.dot(sm_s[hh, rs, :], sbs[hh], preferred_element_type=F32) for hh in heads]
            return tuple(states[hh] * egl_s[hh, es, :][0:1, :] - upd[hh] + sb_s[hh, rs, :] for hh in heads)

        lax.fori_loop(0, nc, step, tuple(jnp.zeros((hd, hd), F32) for _ in heads))

        nw = nw_ref[...]
        pgrp = min(DN_POST_GROUP, nc)

        def post(gi, carry):
            r0 = pl.multiple_of(gi * pgrp * c, pgrp * c)
            rss = [pl.ds(pl.multiple_of(r0 + ci * c, c), c) for ci in range(pgrp)]
            for hh in heads:
                lanes = slice(hh * hd, (hh + 1) * hd)
                os_ = [jnp.dot(qe_s[hh, rs, :], st_s[hh, rs, :], preferred_element_type=F32) + o0_s[hh, rs, :]
                       for rs in rss]
                for rs, o in zip(rss, os_):
                    on = o * lax.rsqrt(jnp.mean(o * o, axis=-1, keepdims=True) + NORM_EPS) * nw
                    o_ref[rs, lanes] = (on * _silu(z_ref[rs, lanes].astype(F32))).astype(o_ref.dtype)
            return carry

        lax.fori_loop(0, nc // pgrp, post, 0)


def _deltanet(proj, ab, conv_w, alog_b, dtb_b, norm_w, l, batch, t):
    n = proj.shape[0]
    hb = DN_HEAD_DIM
    qb, kb_, vb_ = OFF_DQ // hb, OFF_DK // hb, OFF_DV // hb
    col = lambda base: pl.BlockSpec((t, hb), lambda b, h: (b, base + h))
    cw = lambda base: _lspec(l, (CONV_WIDTH, hb), lambda b, h: (0, base + h))
    par = _lspec(l, (None, 1, LANES), lambda b, h: (h, 0, 0))
    return pl.pallas_call(
        _dn_kernel,
        grid=(batch, DN_HEADS),
        in_specs=[
            col(qb), col(kb_), col(vb_),
            pl.BlockSpec((t, DN_WIDTH), lambda b, h: (b, OFF_DZ // DN_WIDTH)),
            pl.BlockSpec((t, LANES), lambda b, h: (b, 0)),
            cw(0), cw(DN_HEADS), cw(2 * DN_HEADS),
            par, par,
            _lspec(l, (1, hb), lambda b, h: (0, 0)),
        ],
        out_specs=pl.BlockSpec((t, DN_WIDTH), lambda b, h: (b, 0)),
        out_shape=jax.ShapeDtypeStruct((n, DN_WIDTH), BF16),
        scratch_shapes=[
            pltpu.VMEM((DN_HEADS, t, hb), BF16),
            pltpu.VMEM((DN_HEADS, t, hb), F32),
            pltpu.VMEM((DN_HEADS, t, hb), BF16),
            pltpu.VMEM((DN_HEADS, t, hb), F32),
            pltpu.VMEM((DN_HEADS, t, hb), BF16),
            pltpu.VMEM((DN_HEADS, t // DN_CHUNK * SUBLANES, LANES), F32),
        ],
        compiler_params=_params(("parallel", "arbitrary")),
        name="deltanet",
    )(proj, proj, proj, proj, ab, conv_w, conv_w, conv_w, alog_b, dtb_b, norm_w)


def _rope(x, cos_t, sin_t):
    half = ROPE_DIMS // 2
    lane = lax.broadcasted_iota(jnp.int32, x.shape, 1)
    swapped = jnp.where(lane < half, pltpu.roll(x, LANES - half, axis=1), pltpu.roll(x, half, axis=1))
    return x * cos_t + swapped * sin_t


MB_Q_GROUP = 4


def _moba_kernel(q_ref, k_ref, v_ref, cos_ref, sin_ref, o_ref, kr_s, vt_s, qs_s):
    t = k_ref.shape[0]
    blk = MB_BLOCK
    nb = t // blk
    blocks = [slice(j * blk, (j + 1) * blk) for j in range(nb)]

    kms = []
    for sl in blocks:
        kr = _rope(k_ref[sl, :].astype(F32), cos_ref[sl, :], sin_ref[sl, :])
        kr_s[sl, :] = kr.astype(BF16)
        kms.append(jnp.mean(kr, axis=0, keepdims=True))
        vt_s[:, sl] = v_ref[sl, :].astype(F32).T.astype(BF16)
    km = jnp.concatenate(kms, axis=0)

    bidx = lax.broadcasted_iota(jnp.int32, (nb, blk), 0)
    sels = []
    for i, sl in enumerate(blocks):
        q = _rope(q_ref[sl, :].astype(F32), cos_ref[sl, :], sin_ref[sl, :])
        qs_s[sl, :] = (q * MB_SCALE).astype(BF16)
        g_t = lax.dot_general(km, q, (((1,), (1,)), ((), ())), precision=HIGHEST,
                              preferred_element_type=F32)
        rank = jnp.zeros((nb, blk), F32)
        for m in range(i):
            gm = g_t[m:m + 1, :]
            rank = rank + jnp.where(gm > g_t, 1.0, jnp.where((gm == g_t) & (bidx > m), 1.0, 0.0))
        sels.append(jnp.where((rank < MB_TOP_K) & (bidx < i), 1.0, 0.0))

    krow = lax.broadcasted_iota(jnp.int32, (blk, blk), 0)
    qcol = lax.broadcasted_iota(jnp.int32, (blk, blk), 1)

    for g0 in range(0, nb, MB_Q_GROUP):
        tiles = list(range(g0, min(g0 + MB_Q_GROUP, nb)))
        ss = {}
        for i in tiles:
            for j in range(i + 1):
                s = lax.dot_general(kr_s[blocks[j], :], qs_s[blocks[i], :], (((1,), (1,)), ((), ())),
                                    preferred_element_type=F32)
                mask = (krow <= qcol) if j == i else (sels[i][j:j + 1, :] > 0.5)
                ss[i, j] = jnp.where(mask, s, -jnp.inf)
        m_all = {i: functools.reduce(jnp.maximum, [jnp.max(ss[i, j], axis=0, keepdims=True)
                                                   for j in range(i + 1)]) for i in tiles}
        ps = {(i, j): jnp.exp(ss[i, j] - m_all[i]) for (i, j) in ss}
        for i in tiles:
            l_all = functools.reduce(jnp.add, [jnp.sum(ps[i, j], axis=0, keepdims=True) for j in range(i + 1)])
            p_all = jnp.concatenate([ps[i, j].astype(BF16) for j in range(i + 1)], axis=0)
            acc = jnp.dot(vt_s[:, :(i + 1) * blk], p_all, preferred_element_type=F32)
            o_ref[blocks[i], :] = (acc / l_all).T.astype(o_ref.dtype)


def _moba(proj, cos_t, sin_t, batch, t):
    n = proj.shape[0]
    hb = MB_HEAD_DIM
    qb, kb_, vb_ = OFF_MQ // hb, OFF_MK // hb, OFF_MV // hb
    full = lambda base: pl.BlockSpec((t, hb), lambda b, h: (b, base + h))
    table = pl.BlockSpec((t, hb), lambda b, h: (0, 0))
    return pl.pallas_call(
        _moba_kernel,
        grid=(batch, MB_HEADS),
        in_specs=[full(qb), full(kb_), full(vb_), table, table],
        out_specs=pl.BlockSpec((t, hb), lambda b, h: (b, h)),
        out_shape=jax.ShapeDtypeStruct((n, MB_WIDTH), BF16),
        scratch_shapes=[
            pltpu.VMEM((t, hb), BF16),
            pltpu.VMEM((hb, t), BF16),
            pltpu.VMEM((t, hb), BF16),
        ],
        compiler_params=_params(("parallel", "parallel")),
        name="moba",
    )(proj, proj, proj, cos_t, sin_t)


LRU_ROWS = 256


def _lru_kernel(x_ref, y_ref, cw_ref, cb_ref, wg_ref, bg_ref, lam_ref, o_ref, a_s, u_s):
    t = x_ref.shape[0]
    w = LRU_WIDTH
    rows = min(LRU_ROWS, t)
    sp = _softplus(-lam_ref[...])

    def gates(ci, carry):
        r0 = pl.multiple_of(ci * rows, rows)
        cur = x_ref[pl.ds(r0, rows), :].astype(F32)
        xc = _causal_conv(cur, _prev_rows(x_ref, r0), cw_ref[...]) + cb_ref[...]
        ri = jnp.dot(xc.astype(BF16), wg_ref[...], preferred_element_type=F32) + bg_ref[...]
        r = _sigmoid(ri[:, :w])
        i = _sigmoid(ri[:, w:])
        log_a = -LRU_C * r * sp
        a = jnp.exp(log_a)
        e2 = jnp.exp(2.0 * log_a)
        em1 = jnp.where(e2 == 1.0, 2.0 * log_a, (e2 - 1.0) * (2.0 * log_a) / jnp.log(e2))
        a_s[pl.ds(r0, rows), :] = a
        u_s[pl.ds(r0, rows), :] = jnp.sqrt(-em1) * (i * xc)
        return carry

    lax.fori_loop(0, t // rows, gates, 0)

    sub = lax.broadcasted_iota(jnp.int32, (SUBLANES, w), 0)

    def scan(ti, hprev):
        rs = pl.ds(pl.multiple_of(ti * SUBLANES, SUBLANES), SUBLANES)
        a = a_s[rs, :]
        u = u_s[rs, :]
        for s in (1, 2, 4):
            keep = sub >= s
            u = jnp.where(keep, a * pltpu.roll(u, s, axis=0) + u, u)
            a = jnp.where(keep, a * pltpu.roll(a, s, axis=0), a)
        hs = a * hprev + u
        o_ref[rs, :] = (hs * _gelu_tanh(y_ref[rs, :].astype(F32))).astype(o_ref.dtype)
        return hs[SUBLANES - 1:SUBLANES, :]

    lax.fori_loop(0, t // SUBLANES, scan, jnp.zeros((1, w), F32))


def _rglru(proj, conv_w, conv_b, w_gates, b_gates, lam, l, batch, t):
    n = proj.shape[0]
    w = LRU_WIDTH
    const = lambda shape: _lspec(l, shape, lambda b: (0, 0))
    return pl.pallas_call(
        _lru_kernel,
        grid=(batch,),
        in_specs=[
            pl.BlockSpec((t, w), lambda b: (b, OFF_LX // w)),
            pl.BlockSpec((t, w), lambda b: (b, OFF_LY // w)),
            const((CONV_WIDTH, w)), const((1, w)), const((w, 2 * w)), const((1, 2 * w)), const((1, w)),
        ],
        out_specs=pl.BlockSpec((t, w), lambda b: (b, 0)),
        out_shape=jax.ShapeDtypeStruct((n, w), BF16),
        scratch_shapes=[pltpu.VMEM((t, w), F32), pltpu.VMEM((t, w), F32)],
        compiler_params=_params(("parallel",)),
        name="rglru",
    )(proj, proj, conv_w, conv_b, w_gates, b_gates, lam)


def _merge_kernel(x_ref, g0_ref, g1_ref, g2_ref, bg_ref, dn_ref, mb_ref, lr_ref,
                  wdn_ref, wmb_ref, wlr_ref, wo_ref, o_ref):
    d = D_MODEL
    bg = bg_ref[...]
    mixed = _sigmoid(g0_ref[...].astype(F32) + bg[:, 0:d]) * jnp.dot(
        dn_ref[...], wdn_ref[...], preferred_element_type=F32)
    mixed = mixed + _sigmoid(g1_ref[...].astype(F32) + bg[:, d:2 * d]) * jnp.dot(
        mb_ref[...], wmb_ref[...], preferred_element_type=F32)
    mixed = mixed + _sigmoid(g2_ref[...].astype(F32) + bg[:, 2 * d:3 * d]) * jnp.dot(
        lr_ref[...], wlr_ref[...], preferred_element_type=F32)
    o_ref[...] = x_ref[...] + jnp.dot(mixed.astype(BF16), wo_ref[...], preferred_element_type=F32)


def _merge(x2, proj, b_gate, y_dn, y_mb, y_lru, w_dn, w_mb, w_lru, w_o, l, tm):
    n, d = x2.shape
    rowblk = lambda width, j: pl.BlockSpec((tm, width), lambda i: (i, j))
    const = lambda shape: _lspec(l, shape, lambda i: (0, 0))
    g0 = OFF_GATES // d
    return pl.pallas_call(
        _merge_kernel,
        grid=(n // tm,),
        in_specs=[
            rowblk(d, 0), rowblk(d, g0), rowblk(d, g0 + 1), rowblk(d, g0 + 2), const((1, N_BRANCH * d)),
            rowblk(DN_WIDTH, 0), rowblk(MB_WIDTH, 0), rowblk(LRU_WIDTH, 0),
            const((DN_WIDTH, d)), const((MB_WIDTH, d)), const((LRU_WIDTH, d)), const((d, d)),
        ],
        out_specs=rowblk(d, 0),
        out_shape=jax.ShapeDtypeStruct((n, d), F32),
        compiler_params=_params(("parallel",)),
        name="merge",
    )(x2, proj, proj, proj, b_gate, y_dn, y_mb, y_lru, w_dn, w_mb, w_lru, w_o)


FFN_CHUNK = 256


def _ffn_kernel(x_ref, g_ref, win_ref, wdn_ref, gf_ref, o_ref, act_s, *, final_norm):
    x = x_ref[...]
    ms = jnp.mean(x * x, axis=-1, keepdims=True)
    hn = (x * lax.rsqrt(ms + NORM_EPS) * g_ref[...]).astype(BF16)
    fc = FFN_CHUNK
    for ci in range(FFN_HIDDEN // fc):
        gu = jnp.dot(hn, win_ref[:, 2 * ci * fc:2 * (ci + 1) * fc], preferred_element_type=F32)
        act_s[:, ci * fc:(ci + 1) * fc] = (_silu(gu[:, :fc]) * gu[:, fc:]).astype(BF16)
    y = x + jnp.dot(act_s[...], wdn_ref[...], preferred_element_type=F32)
    if final_norm:
        ms = jnp.mean(y * y, axis=-1, keepdims=True)
        y = y * lax.rsqrt(ms + NORM_EPS) * gf_ref[...]
    o_ref[...] = y


def _ffn(x2, gain, w_in, w_down, gain_final, l, tm, final_norm):
    n, d = x2.shape
    const = lambda shape: _lspec(l, shape, lambda i: (0, 0))
    return pl.pallas_call(
        functools.partial(_ffn_kernel, final_norm=final_norm),
        grid=(n // tm,),
        in_specs=[
            pl.BlockSpec((tm, d), lambda i: (i, 0)),
            const((1, d)), const((d, 2 * FFN_HIDDEN)), const((FFN_HIDDEN, d)),
            pl.BlockSpec((1, d), lambda i: (0, 0)),
        ],
        out_specs=pl.BlockSpec((tm, d), lambda i: (i, 0)),
        out_shape=jax.ShapeDtypeStruct((n, d), F32),
        scratch_shapes=[pltpu.VMEM((tm, FFN_HIDDEN), BF16)],
        compiler_params=_params(("parallel",)),
        name="ffn",
    )(x2, gain, w_in, w_down, gain_final)


def _rope_tables(t):
    half = ROPE_DIMS // 2
    inv_freq = ROPE_THETA ** (-jnp.arange(half, dtype=F32) / half)
    ang = jnp.arange(t, dtype=F32)[:, None] * inv_freq[None, :]
    cos, sin = jnp.cos(ang), jnp.sin(ang)
    pad = MB_HEAD_DIM - ROPE_DIMS
    cos_t = jnp.concatenate([cos, cos, jnp.ones((t, pad), F32)], axis=1)
    sin_t = jnp.concatenate([-sin, sin, jnp.zeros((t, pad), F32)], axis=1)
    return cos_t, sin_t


def _block_diag(w):
    nl, g, di, do = w.shape
    eye = jnp.eye(g, dtype=w.dtype)
    return (w[:, :, :, None, :] * eye[None, :, None, :, None]).reshape(nl, g * di, g * do)


def _pick_tile(n, pref):
    tile = min(pref, n)
    while n % tile:
        tile //= 2
    return tile


def kernel(x, norm_mix, w_in, b_gate, dn_conv, dn_a_log, dn_dt_bias, dn_norm, dn_out, mb_out, lru_conv_w, lru_conv_b, lru_w_r, lru_b_r, lru_w_i, lru_b_i, lru_lambda, lru_out, w_o, norm_ffn, ffn_in, ffn_down, norm_final):
    batch, t, d = x.shape
    assert d == D_MODEL and t % MB_BLOCK == 0 and t % DN_CHUNK == 0
    n = batch * t
    depth = w_in.shape[0]
    x2 = x.reshape(n, d)
    cos_t, sin_t = _rope_tables(t)
    tm_proj = _pick_tile(n, 1024)
    tm_mix = _pick_tile(n, 512)

    c_ab = 4 * DN_WIDTH
    c_mq = c_ab + 2 * DN_HEADS
    c_g = c_mq + 3 * MB_WIDTH + 2 * LRU_WIDTH

    row3 = lambda p: p[:, None, :]
    w_main = jnp.concatenate([w_in[:, :, c_g:], w_in[:, :, :c_ab], w_in[:, :, c_mq:c_g]], axis=2).astype(BF16)
    w_ab = jnp.pad(w_in[:, :, c_ab:c_mq], ((0, 0), (0, 0), (0, LANES - 2 * DN_HEADS))).astype(BF16)
    alog_b = jnp.broadcast_to(dn_a_log[:, :, None, None], (depth, DN_HEADS, 1, LANES))
    dtb_b = jnp.broadcast_to(dn_dt_bias[:, :, None, None], (depth, DN_HEADS, 1, LANES))
    w_gates = jnp.concatenate([_block_diag(lru_w_r), _block_diag(lru_w_i)], axis=2).astype(BF16)
    b_gates = row3(jnp.concatenate([lru_b_r, lru_b_i], axis=1))
    w_dn, w_mb, w_lru, w_out = (w.astype(BF16) for w in (dn_out, mb_out, lru_out, w_o))
    fc = FFN_CHUNK
    w_ffn_in = ffn_in.reshape(depth, d, 2, FFN_HIDDEN // fc, fc).transpose(0, 1, 3, 2, 4)
    w_ffn_in = w_ffn_in.reshape(depth, d, 2 * FFN_HIDDEN).astype(BF16)
    w_ffn_down = ffn_down.astype(BF16)

    for l in range(depth):
        proj, ab = _norm_proj(x2, row3(norm_mix), w_main, w_ab, l, tm_proj, PROJ_TN)
        y_dn = _deltanet(proj, ab, dn_conv, alog_b, dtb_b, row3(dn_norm), l, batch, t)
        y_mb = _moba(proj, cos_t, sin_t, batch, t)
        y_lru = _rglru(proj, lru_conv_w, row3(lru_conv_b), w_gates, b_gates, row3(lru_lambda), l, batch, t)
        x2 = _merge(x2, proj, row3(b_gate), y_dn, y_mb, y_lru, w_dn, w_mb, w_lru, w_out, l, tm_mix)
        x2 = _ffn(x2, row3(norm_ffn), w_ffn_in, w_ffn_down, norm_final[None, :], l, tm_mix,
                  final_norm=(l == depth - 1))

    return x2.reshape(batch, t, d)
```

```python
import functools

import jax
import jax.numpy as jnp
from jax import lax
from jax.experimental import pallas as pl
from jax.experimental.pallas import tpu as pltpu

F32 = jnp.float32
BF16 = jnp.bfloat16
HIGHEST = lax.Precision.HIGHEST

D_MODEL = 1024
NORM_EPS = 1e-6
CONV_WIDTH = 4
DN_HEADS = 4
DN_HEAD_DIM = 128
DN_WIDTH = DN_HEADS * DN_HEAD_DIM
MB_HEADS = 4
MB_HEAD_DIM = 128
MB_WIDTH = MB_HEADS * MB_HEAD_DIM
MB_BLOCK = 256
MB_TOP_K = 3
MB_SCALE = MB_HEAD_DIM ** -0.5
LOG2_E = 1.4426950408889634
ROPE_THETA = 500000.0
ROPE_DIMS = MB_HEAD_DIM // 4
LRU_WIDTH = 512
LRU_GROUPS = 8
LRU_GROUP_DIM = LRU_WIDTH // LRU_GROUPS
LRU_C = 8.0
N_BRANCH = 3
FFN_HIDDEN = -(-8 * D_MODEL // (3 * 256)) * 256

LANES = 128
SUBLANES = 8
BF16_ROWS = 16
VMEM_LIMIT = 56 * 1024 * 1024

OFF_GATES = 0
OFF_DQ = OFF_GATES + N_BRANCH * D_MODEL
OFF_DK = OFF_DQ + DN_WIDTH
OFF_DV = OFF_DK + DN_WIDTH
OFF_DZ = OFF_DV + DN_WIDTH
OFF_MQ = OFF_DZ + DN_WIDTH
OFF_MK = OFF_MQ + MB_WIDTH
OFF_MV = OFF_MK + MB_WIDTH
OFF_LX = OFF_MV + MB_WIDTH
OFF_LY = OFF_LX + LRU_WIDTH
PROJ_COLS = OFF_LY + LRU_WIDTH
PROJ_TN = 2560

DN_CHUNK = 128
DN_INV_BLOCK = 16
DN_GROUP = 16
DN_POST_GROUP = 8


def _lspec(l, shape, index_map):
    return pl.BlockSpec((None,) + shape, lambda *g: (l,) + index_map(*g))


def _params(sem):
    return pltpu.CompilerParams(dimension_semantics=sem, vmem_limit_bytes=VMEM_LIMIT)


def _sigmoid(x):
    return 1.0 / (1.0 + jnp.exp(-x))


def _silu(x):
    return x * _sigmoid(x)


def _softplus(x):
    return jnp.maximum(x, 0.0) + jnp.log1p(jnp.exp(-jnp.abs(x)))


def _gelu_tanh(x):
    c = (2.0 / jnp.pi) ** 0.5
    return 0.5 * x * (1.0 + jnp.tanh(c * (x + 0.044715 * (x * x * x))))


def _bdot(a, b):
    return jnp.dot(a.astype(BF16), b.astype(BF16), preferred_element_type=F32)


def _bdot_nt(a, b):
    return lax.dot_general(a.astype(BF16), b.astype(BF16), (((1,), (1,)), ((), ())),
                           preferred_element_type=F32)


def _bdot_tn(a, b):
    return lax.dot_general(a.astype(BF16), b.astype(BF16), (((0,), (0,)), ((), ())),
                           preferred_element_type=F32)


def _norm_proj_kernel(x_ref, g_ref, w_ref, wab_ref, o_ref, oab_ref, hn_ref):
    @pl.when(pl.program_id(1) == 0)
    def _():
        x = x_ref[...]
        ms = jnp.mean(x * x, axis=-1, keepdims=True)
        hn = (x * lax.rsqrt(ms + NORM_EPS) * g_ref[...]).astype(BF16)
        hn_ref[...] = hn
        oab_ref[...] = jnp.dot(hn, wab_ref[...], preferred_element_type=F32)

    o_ref[...] = jnp.dot(hn_ref[...], w_ref[...], preferred_element_type=F32).astype(o_ref.dtype)


def _norm_proj(x2, gain, w_main, w_ab, l, tm, tn):
    n, d = x2.shape
    cols = w_main.shape[2]
    return pl.pallas_call(
        _norm_proj_kernel,
        grid=(n // tm, cols // tn),
        in_specs=[
            pl.BlockSpec((tm, d), lambda i, j: (i, 0)),
            _lspec(l, (1, d), lambda i, j: (0, 0)),
            _lspec(l, (d, tn), lambda i, j: (0, j)),
            _lspec(l, (d, LANES), lambda i, j: (0, 0)),
        ],
        out_specs=[
            pl.BlockSpec((tm, tn), lambda i, j: (i, j)),
            pl.BlockSpec((tm, LANES), lambda i, j: (i, 0)),
        ],
        out_shape=[
            jax.ShapeDtypeStruct((n, cols), BF16),
            jax.ShapeDtypeStruct((n, LANES), F32),
        ],
        scratch_shapes=[pltpu.VMEM((tm, d), BF16)],
        compiler_params=_params(("parallel", "arbitrary")),
        name="norm_proj",
    )(x2, gain, w_main, w_ab)


def _unit_lower_inverse(lows, row, col):
    c = lows[0].shape[0]
    same_blk = (row // DN_INV_BLOCK) == (col // DN_INV_BLOCK)
    eye = jnp.where(row == col, 1.0, 0.0).astype(F32)

    def neumann(ns, nilpotency):
        invs = [eye - n for n in ns]
        pws = [_bdot(n, n) for n in ns]
        levels = nilpotency.bit_length() - 2
        for lvl in range(levels):
            if lvl < levels - 1:
                both = [_bdot(pw, jnp.concatenate([pw, inv], axis=1)) for pw, inv in zip(pws, invs)]
                pws = [b[:, :c] for b in both]
                invs = [inv + b[:, c:] for inv, b in zip(invs, both)]
            else:
                invs = [inv + _bdot(pw, inv) for pw, inv in zip(pws, invs)]
        return invs

    dgs = [jnp.where(same_blk, low, 0.0) for low in lows]
    offs = [low - dg for low, dg in zip(lows, dgs)]
    inv_ds = neumann(dgs, DN_INV_BLOCK)
    inv_ms = neumann([_bdot(inv_d, off) for inv_d, off in zip(inv_ds, offs)], c // DN_INV_BLOCK)
    return [_bdot(inv_m, inv_d) for inv_m, inv_d in zip(inv_ms, inv_ds)]


def _dn_kernel(q_ref, k_ref, v_ref, z_ref, ab_ref, cwq_ref, cwk_ref, cwv_ref, alog_ref, dtb_ref, nw_ref,
               o_ref, sm_s, sb_s, qe_s, o0_s, st_s, egl_s):
    t = q_ref.shape[0]
    c = DN_CHUNK
    hd = DN_HEAD_DIM
    nc = t // c
    grp = min(DN_GROUP, nc)
    rows_g = grp * c
    h = pl.program_id(1)

    row = lax.broadcasted_iota(jnp.int32, (c, c), 0)
    col = lax.broadcasted_iota(jnp.int32, (c, c), 1)
    r2 = lax.broadcasted_iota(jnp.int32, (c, 2 * c), 0)
    c2 = lax.broadcasted_iota(jnp.int32, (c, 2 * c), 1)
    tril2 = jnp.where(r2 >= c2 % c, 1.0, 0.0).astype(BF16)
    pr = lax.broadcasted_iota(jnp.int32, (2 * LANES, 2 * LANES), 0) % LANES
    pc = lax.broadcasted_iota(jnp.int32, (2 * LANES, 2 * LANES), 1)
    pick2 = jnp.where(pr == jnp.where(pc < LANES, h, h + DN_HEADS), 1.0, 0.0).astype(BF16)
    sr = lax.broadcasted_iota(jnp.int32, ((CONV_WIDTH - 1) * c, 2 * c), 0)
    sc = lax.broadcasted_iota(jnp.int32, ((CONV_WIDTH - 1) * c, 2 * c), 1)
    shift_m = jnp.where(sc == c + sr % c - (sr // c + 1), 1.0, 0.0).astype(BF16)
    neg_a = -jnp.exp(alog_ref[...])
    dtb = dtb_ref[...]
    cw = jnp.concatenate([cwq_ref[...], cwk_ref[...], cwv_ref[...]], axis=1)

    def split2(x, axis):
        hi = x.astype(BF16)
        lo = (x - hi.astype(F32)).astype(BF16)
        return jnp.concatenate([hi, lo], axis=axis)

    def l2n(x):
        return x * lax.rsqrt(jnp.sum(x * x, axis=-1, keepdims=True) + NORM_EPS)

    def prep(gi, carry):
        r0 = pl.multiple_of(gi * rows_g, rows_g)
        rows = pl.ds(r0, rows_g)
        before = pl.ds(pl.multiple_of(jnp.maximum(r0 - c, 0), c), c)
        xg = jnp.concatenate([q_ref[rows, :], k_ref[rows, :], v_ref[rows, :]], axis=1).astype(BF16)
        xp = jnp.concatenate([q_ref[before, :], k_ref[before, :], v_ref[before, :]], axis=1).astype(BF16)
        xp = jnp.where(r0 > 0, xp, jnp.zeros_like(xp))
        ab2 = split2(ab_ref[rows, :], 1)
        agb = jnp.dot(ab2, pick2, preferred_element_type=F32)
        g_b = neg_a * _softplus(agb[:, :LANES] + dtb)
        beta_b = _sigmoid(agb[:, LANES:])
        rng = range(grp)
        sls = [slice(ci * c, (ci + 1) * c) for ci in rng]
        xcs = [xg[s] for s in sls]
        xxs = [jnp.concatenate([xp if ci == 0 else xcs[ci - 1], xcs[ci]], axis=0) for ci in rng]
        shs = [jnp.dot(shift_m, xx, preferred_element_type=F32) for xx in xxs]
        convs = [xc.astype(F32) * cw[CONV_WIDTH - 1:CONV_WIDTH, :]
                 + sh[0:c] * cw[2:3, :] + sh[c:2 * c] * cw[1:2, :] + sh[2 * c:3 * c] * cw[0:1, :]
                 for xc, sh in zip(xcs, shs)]
        acts = [_silu(cv) for cv in convs]
        qcs = [l2n(a[:, :hd]) * (DN_HEAD_DIM ** -0.5) for a in acts]
        kcs = [l2n(a[:, hd:2 * hd]) for a in acts]
        vcs = [a[:, 2 * hd:] for a in acts]
        betas = [beta_b[s] for s in sls]
        gcbs = [jnp.dot(tril2, split2(g_b[s], 0), preferred_element_type=F32) for s in sls]
        gcrs = [gcb.T for gcb in gcbs]
        decays = [jnp.exp(jnp.where(row >= col, gcb - gcr, -jnp.inf)) for gcb, gcr in zip(gcbs, gcrs)]
        kbs = [kc * beta for kc, beta in zip(kcs, betas)]
        lows = [jnp.where(row > col, _bdot_nt(kb, kc) * decay, 0.0) for kb, kc, decay in zip(kbs, kcs, decays)]
        t_invs = _unit_lower_inverse(lows, row, col)
        egs = [jnp.exp(gcb) for gcb in gcbs]
        wus = [_bdot(t_inv, jnp.concatenate([kb * eg, vc * beta], axis=1))
               for t_inv, kb, eg, vc, beta in zip(t_invs, kbs, egs, vcs, betas)]
        attns = [_bdot_nt(qc, kc) * decay for qc, kc, decay in zip(qcs, kcs, decays)]
        gls = [gcb[c - 1:c, :] for gcb in gcbs]
        kes = [kc * jnp.exp(gl - gcb) for kc, gl, gcb in zip(kcs, gls, gcbs)]
        kws = [_bdot_tn(ke, wu) for ke, wu in zip(kes, wus)]
        aws = [_bdot(attn, wu) for attn, wu in zip(attns, wus)]
        for ci in rng:
            rs = pl.ds(pl.multiple_of(r0 + ci * c, c), c)
            sm_s[h, rs, :] = kws[ci][:, :hd].astype(BF16)
            sb_s[h, rs, :] = kws[ci][:, hd:]
            qe_s[h, rs, :] = (qcs[ci] * egs[ci] - aws[ci][:, :hd]).astype(BF16)
            o0_s[h, rs, :] = aws[ci][:, hd:]
            es = pl.ds(pl.multiple_of((gi * grp + ci) * SUBLANES, SUBLANES), SUBLANES)
            egl_s[h, es, :] = jnp.broadcast_to(jnp.exp(gls[ci]), (SUBLANES, LANES))
        return carry

    lax.fori_loop(0, nc // grp, prep, 0)

    @pl.when(h == DN_HEADS - 1)
    def _():
        heads = range(DN_HEADS)

        def step(ci, states):
            rs = pl.ds(pl.multiple_of(ci * c, c), c)
            es = pl.ds(pl.multiple_of(ci * SUBLANES, SUBLANES), SUBLANES)
            sbs = [s.astype(BF16) for s in states]
            for hh in heads:
                st_s[hh, rs, :] = sbs[hh]
            upd = [jnp.dot(sm_s[hh, rs, :], sbs[hh], preferred_element_type=F32) for hh in heads]
            return tuple(states[hh] * egl_s[hh, es, :][0:1, :] - upd[hh] + sb_s[hh, rs, :] for hh in heads)

        lax.fori_loop(0, nc, step, tuple(jnp.zeros((hd, hd), F32) for _ in heads))

        nw = nw_ref[...]
        pgrp = min(DN_POST_GROUP, nc)

        def post(gi, carry):
            r0 = pl.multiple_of(gi * pgrp * c, pgrp * c)
            rss = [pl.ds(pl.multiple_of(r0 + ci * c, c), c) for ci in range(pgrp)]
            for hh in heads:
                lanes = slice(hh * hd, (hh + 1) * hd)
                os_ = [jnp.dot(qe_s[hh, rs, :], st_s[hh, rs, :], preferred_element_type=F32) + o0_s[hh, rs, :]
                       for rs in rss]
                for rs, o in zip(rss, os_):
                    on = o * lax.rsqrt(jnp.mean(o * o, axis=-1, keepdims=True) + NORM_EPS) * nw
                    o_ref[rs, lanes] = (on * _silu(z_ref[rs, lanes].astype(F32))).astype(o_ref.dtype)
            return carry

        lax.fori_loop(0, nc // pgrp, post, 0)


def _deltanet(proj, ab, conv_w, alog_b, dtb_b, norm_w, l, batch, t):
    n = proj.shape[0]
    hb = DN_HEAD_DIM
    qb, kb_, vb_ = OFF_DQ // hb, OFF_DK // hb, OFF_DV // hb
    col = lambda base: pl.BlockSpec((t, hb), lambda b, h: (b, base + h))
    cw = lambda base: _lspec(l, (CONV_WIDTH, hb), lambda b, h: (0, base + h))
    par = _lspec(l, (None, 1, LANES), lambda b, h: (h, 0, 0))
    return pl.pallas_call(
        _dn_kernel,
        grid=(batch, DN_HEADS),
        in_specs=[
            col(qb), col(kb_), col(vb_),
            pl.BlockSpec((t, DN_WIDTH), lambda b, h: (b, OFF_DZ // DN_WIDTH)),
            pl.BlockSpec((t, LANES), lambda b, h: (b, 0)),
            cw(0), cw(DN_HEADS), cw(2 * DN_HEADS),
            par, par,
            _lspec(l, (1, hb), lambda b, h: (0, 0)),
        ],
        out_specs=pl.BlockSpec((t, DN_WIDTH), lambda b, h: (b, 0)),
        out_shape=jax.ShapeDtypeStruct((n, DN_WIDTH), BF16),
        scratch_shapes=[
            pltpu.VMEM((DN_HEADS, t, hb), BF16),
            pltpu.VMEM((DN_HEADS, t, hb), F32),
            pltpu.VMEM((DN_HEADS, t, hb), BF16),
            pltpu.VMEM((DN_HEADS, t, hb), F32),
            pltpu.VMEM((DN_HEADS, t, hb), BF16),
            pltpu.VMEM((DN_HEADS, t // DN_CHUNK * SUBLANES, LANES), F32),
        ],
        compiler_params=_params(("parallel", "arbitrary")),
        name="deltanet",
    )(proj, proj, proj, proj, ab, conv_w, conv_w, conv_w, alog_b, dtb_b, norm_w)


def _rope(x, cos_t, sin_t):
    half = ROPE_DIMS // 2
    lane = lax.broadcasted_iota(jnp.int32, x.shape, 1)
    swapped = jnp.where(lane < half, pltpu.roll(x, LANES - half, axis=1), pltpu.roll(x, half, axis=1))
    return x * cos_t + swapped * sin_t


MB_Q_GROUP = 4


def _moba_kernel(q_ref, k_ref, v_ref, cos_ref, sin_ref, o_ref, kr_s, vt_s, qs_s):
    t = k_ref.shape[0]
    blk = MB_BLOCK
    nb = t // blk
    blocks = [slice(j * blk, (j + 1) * blk) for j in range(nb)]

    kms = []
    for sl in blocks:
        kr = _rope(k_ref[sl, :].astype(F32), cos_ref[sl, :], sin_ref[sl, :])
        kr_s[sl, :] = kr.astype(BF16)
        kms.append(jnp.mean(kr, axis=0, keepdims=True))
        vt_s[:, sl] = v_ref[sl, :].astype(F32).T.astype(BF16)
    km = jnp.concatenate(kms, axis=0)

    bidx = lax.broadcasted_iota(jnp.int32, (nb, blk), 0)
    sels = []
    for i, sl in enumerate(blocks):
        q = _rope(q_ref[sl, :].astype(F32), cos_ref[sl, :], sin_ref[sl, :])
        qs_s[sl, :] = (q * (MB_SCALE * LOG2_E)).astype(BF16)
        g_t = lax.dot_general(km, q, (((1,), (1,)), ((), ())), precision=HIGHEST,
                              preferred_element_type=F32)
        rank = jnp.zeros((nb, blk), F32)
        for m in range(i):
            gm = g_t[m:m + 1, :]
            rank = rank + jnp.where(gm > g_t, 1.0, jnp.where((gm == g_t) & (bidx > m), 1.0, 0.0))
        sels.append(jnp.where((rank < MB_TOP_K) & (bidx < i), 1.0, 0.0))

    krow = lax.broadcasted_iota(jnp.int32, (blk, blk), 0)
    qcol = lax.broadcasted_iota(jnp.int32, (blk, blk), 1)

    for g0 in range(0, nb, MB_Q_GROUP):
        tiles = list(range(g0, min(g0 + MB_Q_GROUP, nb)))
        ss = {}
        for i in tiles:
            for j in range(i + 1):
                s = lax.dot_general(kr_s[blocks[j], :], qs_s[blocks[i], :], (((1,), (1,)), ((), ())),
                                    preferred_element_type=F32)
                mask = (krow <= qcol) if j == i else (sels[i][j:j + 1, :] > 0.5)
                ss[i, j] = jnp.where(mask, s, -jnp.inf)
        m_all = {i: functools.reduce(jnp.maximum, [jnp.max(ss[i, j], axis=0, keepdims=True)
                                                   for j in range(i + 1)]) for i in tiles}
        ps = {(i, j): jnp.exp2(ss[i, j] - m_all[i]) for (i, j) in ss}
        for i in tiles:
            l_all = functools.reduce(jnp.add, [jnp.sum(ps[i, j], axis=0, keepdims=True) for j in range(i + 1)])
            p_all = jnp.concatenate([ps[i, j].astype(BF16) for j in range(i + 1)], axis=0)
            acc = jnp.dot(vt_s[:, :(i + 1) * blk], p_all, preferred_element_type=F32)
            o_ref[blocks[i], :] = (acc / l_all).T.astype(o_ref.dtype)


def _moba(proj, cos_t, sin_t, batch, t):
    n = proj.shape[0]
    hb = MB_HEAD_DIM
    qb, kb_, vb_ = OFF_MQ // hb, OFF_MK // hb, OFF_MV // hb
    full = lambda base: pl.BlockSpec((t, hb), lambda b, h: (b, base + h))
    table = pl.BlockSpec((t, hb), lambda b, h: (0, 0))
    return pl.pallas_call(
        _moba_kernel,
        grid=(batch, MB_HEADS),
        in_specs=[full(qb), full(kb_), full(vb_), table, table],
        out_specs=pl.BlockSpec((t, hb), lambda b, h: (b, h)),
        out_shape=jax.ShapeDtypeStruct((n, MB_WIDTH), BF16),
        scratch_shapes=[
            pltpu.VMEM((t, hb), BF16),
            pltpu.VMEM((hb, t), BF16),
            pltpu.VMEM((t, hb), BF16),
        ],
        compiler_params=_params(("parallel", "parallel")),
        name="moba",
    )(proj, proj, proj, cos_t, sin_t)


LRU_ROWS = 256


def _causal_conv(cur, prev, w):
    ext = jnp.concatenate([prev, cur], axis=0)
    out = cur * w[CONV_WIDTH - 1:CONV_WIDTH, :]
    for s in range(1, CONV_WIDTH):
        shifted = pltpu.roll(ext, s, axis=0)[SUBLANES:, :]
        out = out + shifted * w[CONV_WIDTH - 1 - s:CONV_WIDTH - s, :]
    return out


def _prev_rows(ref, r0):
    start = pl.multiple_of(jnp.maximum(r0 - BF16_ROWS, 0), BF16_ROWS)
    rows = ref[pl.ds(start, BF16_ROWS), :].astype(F32)[BF16_ROWS - SUBLANES:, :]
    return jnp.where(r0 > 0, rows, 0.0)


def _lru_kernel(x_ref, y_ref, cw_ref, cb_ref, wg_ref, bg_ref, lam_ref, o_ref, a_s, u_s):
    t = x_ref.shape[0]
    w = LRU_WIDTH
    rows = min(LRU_ROWS, t)
    neg_c_sp = -LRU_C * _softplus(-lam_ref[...])

    def gates(ci, carry):
        r0 = pl.multiple_of(ci * rows, rows)
        cur = x_ref[pl.ds(r0, rows), :].astype(F32)
        xc = _causal_conv(cur, _prev_rows(x_ref, r0), cw_ref[...]) + cb_ref[...]
        ri = jnp.dot(xc.astype(BF16), wg_ref[...], preferred_element_type=F32) + bg_ref[...]
        r = _sigmoid(ri[:, :w])
        i = _sigmoid(ri[:, w:])
        log_a = r * neg_c_sp
        a = jnp.exp(log_a)
        gain2 = jnp.tanh(-log_a) * (1.0 + a * a)
        a_s[pl.ds(r0, rows), :] = a
        u_s[pl.ds(r0, rows), :] = jnp.sqrt(gain2) * (i * xc)
        return carry

    lax.fori_loop(0, t // rows, gates, 0)

    sub = lax.broadcasted_iota(jnp.int32, (SUBLANES, w), 0)

    def scan(ti, hprev):
        rs = pl.ds(pl.multiple_of(ti * SUBLANES, SUBLANES), SUBLANES)
        a = a_s[rs, :]
        u = u_s[rs, :]
        for s in (1, 2, 4):
            keep = sub >= s
            u = jnp.where(keep, a * pltpu.roll(u, s, axis=0) + u, u)
            a = jnp.where(keep, a * pltpu.roll(a, s, axis=0), a)
        hs = a * hprev + u
        o_ref[rs, :] = (hs * _gelu_tanh(y_ref[rs, :].astype(F32))).astype(o_ref.dtype)
        return hs[SUBLANES - 1:SUBLANES, :]

    lax.fori_loop(0, t // SUBLANES, scan, jnp.zeros((1, w), F32), unroll=4)


def _rglru(proj, conv_w, conv_b, w_gates, b_gates, lam, l, batch, t):
    n = proj.shape[0]
    w = LRU_WIDTH
    const = lambda shape: _lspec(l, shape, lambda b: (0, 0))
    return pl.pallas_call(
        _lru_kernel,
        grid=(batch,),
        in_specs=[
            pl.BlockSpec((t, w), lambda b: (b, OFF_LX // w)),
            pl.BlockSpec((t, w), lambda b: (b, OFF_LY // w)),
            const((CONV_WIDTH, w)), const((1, w)), const((w, 2 * w)), const((1, 2 * w)), const((1, w)),
        ],
        out_specs=pl.BlockSpec((t, w), lambda b: (b, 0)),
        out_shape=jax.ShapeDtypeStruct((n, w), BF16),
        scratch_shapes=[pltpu.VMEM((t, w), F32), pltpu.VMEM((t, w), F32)],
        compiler_params=_params(("parallel",)),
        name="rglru",
    )(proj, proj, conv_w, conv_b, w_gates, b_gates, lam)


def _merge_kernel(x_ref, g0_ref, g1_ref, g2_ref, bg_ref, dn_ref, mb_ref, lr_ref,
                  wdn_ref, wmb_ref, wlr_ref, wo_ref, o_ref):
    d = D_MODEL
    bg = bg_ref[...]
    mixed = _sigmoid(g0_ref[...].astype(F32) + bg[:, 0:d]) * jnp.dot(
        dn_ref[...], wdn_ref[...], preferred_element_type=F32)
    mixed = mixed + _sigmoid(g1_ref[...].astype(F32) + bg[:, d:2 * d]) * jnp.dot(
        mb_ref[...], wmb_ref[...], preferred_element_type=F32)
    mixed = mixed + _sigmoid(g2_ref[...].astype(F32) + bg[:, 2 * d:3 * d]) * jnp.dot(
        lr_ref[...], wlr_ref[...], preferred_element_type=F32)
    o_ref[...] = x_ref[...] + jnp.dot(mixed.astype(BF16), wo_ref[...], preferred_element_type=F32)


def _merge(x2, proj, b_gate, y_dn, y_mb, y_lru, w_dn, w_mb, w_lru, w_o, l, tm):
    n, d = x2.shape
    rowblk = lambda width, j: pl.BlockSpec((tm, width), lambda i: (i, j))
    const = lambda shape: _lspec(l, shape, lambda i: (0, 0))
    g0 = OFF_GATES // d
    return pl.pallas_call(
        _merge_kernel,
        grid=(n // tm,),
        in_specs=[
            rowblk(d, 0), rowblk(d, g0), rowblk(d, g0 + 1), rowblk(d, g0 + 2), const((1, N_BRANCH * d)),
            rowblk(DN_WIDTH, 0), rowblk(MB_WIDTH, 0), rowblk(LRU_WIDTH, 0),
            const((DN_WIDTH, d)), const((MB_WIDTH, d)), const((LRU_WIDTH, d)), const((d, d)),
        ],
        out_specs=rowblk(d, 0),
        out_shape=jax.ShapeDtypeStruct((n, d), F32),
        compiler_params=_params(("parallel",)),
        name="merge",
    )(x2, proj, proj, proj, b_gate, y_dn, y_mb, y_lru, w_dn, w_mb, w_lru, w_o)


FFN_CHUNK = 256


def _ffn_kernel(x_ref, g_ref, wg_ref, wu_ref, wdn_ref, gf_ref, o_ref, act_s, *, final_norm):
    x = x_ref[...]
    ms = jnp.mean(x * x, axis=-1, keepdims=True)
    hn = (x * lax.rsqrt(ms + NORM_EPS) * g_ref[...]).astype(BF16)
    fc = FFN_CHUNK
    for ci in range(FFN_HIDDEN // fc):
        cols = slice(ci * fc, (ci + 1) * fc)
        gate = jnp.dot(hn, wg_ref[:, cols], preferred_element_type=F32)
        up = jnp.dot(hn, wu_ref[:, cols], preferred_element_type=F32)
        act_s[:, cols] = (_silu(gate) * up).astype(BF16)
    y = x + jnp.dot(act_s[...], wdn_ref[...], preferred_element_type=F32)
    if final_norm:
        ms = jnp.mean(y * y, axis=-1, keepdims=True)
        y = y * lax.rsqrt(ms + NORM_EPS) * gf_ref[...]
    o_ref[...] = y


def _ffn(x2, gain, w_in, w_down, gain_final, l, tm, final_norm):
    n, d = x2.shape
    const = lambda shape: _lspec(l, shape, lambda i: (0, 0))
    return pl.pallas_call(
        functools.partial(_ffn_kernel, final_norm=final_norm),
        grid=(n // tm,),
        in_specs=[
            pl.BlockSpec((tm, d), lambda i: (i, 0)),
            const((1, d)),
            _lspec(l, (d, FFN_HIDDEN), lambda i: (0, 0)),
            _lspec(l, (d, FFN_HIDDEN), lambda i: (0, 1)),
            const((FFN_HIDDEN, d)),
            pl.BlockSpec((1, d), lambda i: (0, 0)),
        ],
        out_specs=pl.BlockSpec((tm, d), lambda i: (i, 0)),
        out_shape=jax.ShapeDtypeStruct((n, d), F32),
        scratch_shapes=[pltpu.VMEM((tm, FFN_HIDDEN), BF16)],
        compiler_params=_params(("parallel",)),
        name="ffn",
    )(x2, gain, w_in, w_in, w_down, gain_final)


def _rope_tables(t):
    half = ROPE_DIMS // 2
    inv_freq = ROPE_THETA ** (-jnp.arange(half, dtype=F32) / half)
    ang = jnp.arange(t, dtype=F32)[:, None] * inv_freq[None, :]
    cos, sin = jnp.cos(ang), jnp.sin(ang)
    pad = MB_HEAD_DIM - ROPE_DIMS
    cos_t = jnp.concatenate([cos, cos, jnp.ones((t, pad), F32)], axis=1)
    sin_t = jnp.concatenate([-sin, sin, jnp.zeros((t, pad), F32)], axis=1)
    return cos_t, sin_t


def _block_diag(w):
    nl, g, di, do = w.shape
    eye = jnp.eye(g, dtype=w.dtype)
    return (w[:, :, :, None, :] * eye[None, :, None, :, None]).reshape(nl, g * di, g * do)


def _pick_tile(n, pref):
    tile = min(pref, n)
    while n % tile:
        tile //= 2
    return tile


def kernel(x, norm_mix, w_in, b_gate, dn_conv, dn_a_log, dn_dt_bias, dn_norm, dn_out, mb_out, lru_conv_w, lru_conv_b, lru_w_r, lru_b_r, lru_w_i, lru_b_i, lru_lambda, lru_out, w_o, norm_ffn, ffn_in, ffn_down, norm_final):
    batch, t, d = x.shape
    assert d == D_MODEL and t % MB_BLOCK == 0 and t % DN_CHUNK == 0
    n = batch * t
    depth = w_in.shape[0]
    x2 = x.reshape(n, d)
    cos_t, sin_t = _rope_tables(t)
    tm_proj = _pick_tile(n, 1024)
    tm_mix = _pick_tile(n, 512)

    c_ab = 4 * DN_WIDTH
    c_mq = c_ab + 2 * DN_HEADS
    c_g = c_mq + 3 * MB_WIDTH + 2 * LRU_WIDTH

    row3 = lambda p: p[:, None, :]
    w_main = jnp.concatenate([w_in[:, :, c_g:], w_in[:, :, :c_ab], w_in[:, :, c_mq:c_g]], axis=2).astype(BF16)
    w_ab = jnp.pad(w_in[:, :, c_ab:c_mq], ((0, 0), (0, 0), (0, LANES - 2 * DN_HEADS))).astype(BF16)
    alog_b = jnp.broadcast_to(dn_a_log[:, :, None, None], (depth, DN_HEADS, 1, LANES))
    dtb_b = jnp.broadcast_to(dn_dt_bias[:, :, None, None], (depth, DN_HEADS, 1, LANES))
    w_gates = jnp.concatenate([_block_diag(lru_w_r), _block_diag(lru_w_i)], axis=2).astype(BF16)
    b_gates = row3(jnp.concatenate([lru_b_r, lru_b_i], axis=1))
    w_dn, w_mb, w_lru, w_out = (w.astype(BF16) for w in (dn_out, mb_out, lru_out, w_o))
    w_ffn_in = ffn_in.astype(BF16)
    w_ffn_down = ffn_down.astype(BF16)

    for l in range(depth):
        proj, ab = _norm_proj(x2, row3(norm_mix), w_main, w_ab, l, tm_proj, PROJ_TN)
        y_dn = _deltanet(proj, ab, dn_conv, alog_b, dtb_b, row3(dn_norm), l, batch, t)
        y_mb = _moba(proj, cos_t, sin_t, batch, t)
        y_lru = _rglru(proj, lru_conv_w, row3(lru_conv_b), w_gates, b_gates, row3(lru_lambda), l, batch, t)
        x2 = _merge(x2, proj, row3(b_gate), y_dn, y_mb, y_lru, w_dn, w_mb, w_lru, w_out, l, tm_mix)
        x2 = _ffn(x2, row3(norm_ffn), w_ffn_in, w_ffn_down, norm_final[None, :], l, tm_mix,
                  final_norm=(l == depth - 1))

    return x2.reshape(batch, t, d)
```

```python
import functools

import jax
import jax.numpy as jnp
from jax import lax
from jax.experimental import pallas as pl
from jax.experimental.pallas import tpu as pltpu

F32 = jnp.float32
BF16 = jnp.bfloat16
HIGHEST = lax.Precision.HIGHEST

D_MODEL = 1024
NORM_EPS = 1e-6
CONV_WIDTH = 4
DN_HEADS = 4
DN_HEAD_DIM = 128
DN_WIDTH = DN_HEADS * DN_HEAD_DIM
MB_HEADS = 4
MB_HEAD_DIM = 128
MB_WIDTH = MB_HEADS * MB_HEAD_DIM
MB_BLOCK = 256
MB_TOP_K = 3
MB_SCALE = MB_HEAD_DIM ** -0.5
LOG2_E = 1.4426950408889634
ROPE_THETA = 500000.0
ROPE_DIMS = MB_HEAD_DIM // 4
LRU_WIDTH = 512
LRU_GROUPS = 8
LRU_GROUP_DIM = LRU_WIDTH // LRU_GROUPS
LRU_C = 8.0
N_BRANCH = 3
FFN_HIDDEN = -(-8 * D_MODEL // (3 * 256)) * 256

LANES = 128
SUBLANES = 8
BF16_ROWS = 16
VMEM_LIMIT = 56 * 1024 * 1024

OFF_GATES = 0
OFF_DQ = OFF_GATES + N_BRANCH * D_MODEL
OFF_DK = OFF_DQ + DN_WIDTH
OFF_DV = OFF_DK + DN_WIDTH
OFF_DZ = OFF_DV + DN_WIDTH
OFF_MQ = OFF_DZ + DN_WIDTH
OFF_MK = OFF_MQ + MB_WIDTH
OFF_MV = OFF_MK + MB_WIDTH
OFF_LX = OFF_MV + MB_WIDTH
OFF_LY = OFF_LX + LRU_WIDTH
PROJ_COLS = OFF_LY + LRU_WIDTH
PROJ_TN = 2560

DN_CHUNK = 128
DN_INV_BLOCK = 16
DN_GROUP = 16
DN_POST_GROUP = 8


def _lspec(l, shape, index_map, single=False):
    mode = dict(pipeline_mode=pl.Buffered(1)) if single else {}
    return pl.BlockSpec((None,) + shape, lambda *g: (l,) + index_map(*g), **mode)


def _params(sem):
    return pltpu.CompilerParams(dimension_semantics=sem, vmem_limit_bytes=VMEM_LIMIT)


def _sigmoid(x):
    return 1.0 / (1.0 + jnp.exp(-x))


def _silu(x):
    return x * _sigmoid(x)


def _softplus(x):
    return jnp.maximum(x, 0.0) + jnp.log1p(jnp.exp(-jnp.abs(x)))


def _gelu_tanh(x):
    c = (2.0 / jnp.pi) ** 0.5
    return 0.5 * x * (1.0 + jnp.tanh(c * (x + 0.044715 * (x * x * x))))


def _bdot(a, b):
    return jnp.dot(a.astype(BF16), b.astype(BF16), preferred_element_type=F32)


def _bdot_nt(a, b):
    return lax.dot_general(a.astype(BF16), b.astype(BF16), (((1,), (1,)), ((), ())),
                           preferred_element_type=F32)


def _bdot_tn(a, b):
    return lax.dot_general(a.astype(BF16), b.astype(BF16), (((0,), (0,)), ((), ())),
                           preferred_element_type=F32)


def _norm_proj_kernel(x_ref, g_ref, w_ref, wab_ref, o_ref, oab_ref, hn_ref):
    @pl.when(pl.program_id(1) == 0)
    def _():
        x = x_ref[...]
        ms = jnp.mean(x * x, axis=-1, keepdims=True)
        hn = (x * lax.rsqrt(ms + NORM_EPS) * g_ref[...]).astype(BF16)
        hn_ref[...] = hn
        oab_ref[...] = jnp.dot(hn, wab_ref[...], preferred_element_type=F32)

    o_ref[...] = jnp.dot(hn_ref[...], w_ref[...], preferred_element_type=F32).astype(o_ref.dtype)


def _norm_proj(x2, gain, w_main, w_ab, l, tm, tn):
    n, d = x2.shape
    cols = w_main.shape[2]
    return pl.pallas_call(
        _norm_proj_kernel,
        grid=(n // tm, cols // tn),
        in_specs=[
            pl.BlockSpec((tm, d), lambda i, j: (i, 0)),
            _lspec(l, (1, d), lambda i, j: (0, 0)),
            _lspec(l, (d, tn), lambda i, j: (0, j)),
            _lspec(l, (d, LANES), lambda i, j: (0, 0)),
        ],
        out_specs=[
            pl.BlockSpec((tm, tn), lambda i, j: (i, j)),
            pl.BlockSpec((tm, LANES), lambda i, j: (i, 0)),
        ],
        out_shape=[
            jax.ShapeDtypeStruct((n, cols), BF16),
            jax.ShapeDtypeStruct((n, LANES), F32),
        ],
        scratch_shapes=[pltpu.VMEM((tm, d), BF16)],
        compiler_params=_params(("parallel", "arbitrary")),
        name="norm_proj",
    )(x2, gain, w_main, w_ab)


def _unit_lower_inverse(lows, row, col):
    c = lows[0].shape[0]
    same_blk = (row // DN_INV_BLOCK) == (col // DN_INV_BLOCK)
    eye = jnp.where(row == col, 1.0, 0.0).astype(F32)

    def neumann(ns, nilpotency):
        invs = [eye - n for n in ns]
        pws = [_bdot(n, n) for n in ns]
        levels = nilpotency.bit_length() - 2
        for lvl in range(levels):
            if lvl < levels - 1:
                both = [_bdot(pw, jnp.concatenate([pw, inv], axis=1)) for pw, inv in zip(pws, invs)]
                pws = [b[:, :c] for b in both]
                invs = [inv + b[:, c:] for inv, b in zip(invs, both)]
            else:
                invs = [inv + _bdot(pw, inv) for pw, inv in zip(pws, invs)]
        return invs

    dgs = [jnp.where(same_blk, low, 0.0) for low in lows]
    offs = [low - dg for low, dg in zip(lows, dgs)]
    inv_ds = neumann(dgs, DN_INV_BLOCK)
    inv_ms = neumann([_bdot(inv_d, off) for inv_d, off in zip(inv_ds, offs)], c // DN_INV_BLOCK)
    return [_bdot(inv_m, inv_d) for inv_m, inv_d in zip(inv_ms, inv_ds)]


def _dn_kernel(q_ref, k_ref, v_ref, z_ref, ab_ref, cwq_ref, cwk_ref, cwv_ref, alog_ref, dtb_ref, nw_ref,
               o_ref, sm_s, sb_s, qe_s, o0_s, st_s, egl_s):
    t = q_ref.shape[0]
    c = DN_CHUNK
    hd = DN_HEAD_DIM
    nc = t // c
    grp = min(DN_GROUP, nc)
    rows_g = grp * c
    h = pl.program_id(1)

    row = lax.broadcasted_iota(jnp.int32, (c, c), 0)
    col = lax.broadcasted_iota(jnp.int32, (c, c), 1)
    r2 = lax.broadcasted_iota(jnp.int32, (c, 2 * c), 0)
    c2 = lax.broadcasted_iota(jnp.int32, (c, 2 * c), 1)
    tril2 = jnp.where(r2 >= c2 % c, 1.0, 0.0).astype(BF16)
    pr = lax.broadcasted_iota(jnp.int32, (2 * LANES, 2 * LANES), 0) % LANES
    pc = lax.broadcasted_iota(jnp.int32, (2 * LANES, 2 * LANES), 1)
    pick2 = jnp.where(pr == jnp.where(pc < LANES, h, h + DN_HEADS), 1.0, 0.0).astype(BF16)
    sr = lax.broadcasted_iota(jnp.int32, ((CONV_WIDTH - 1) * c, 2 * c), 0)
    sc = lax.broadcasted_iota(jnp.int32, ((CONV_WIDTH - 1) * c, 2 * c), 1)
    shift_m = jnp.where(sc == c + sr % c - (sr // c + 1), 1.0, 0.0).astype(BF16)
    neg_a = -jnp.exp(alog_ref[...])
    dtb = dtb_ref[...]
    cw = jnp.concatenate([cwq_ref[...], cwk_ref[...], cwv_ref[...]], axis=1)

    def split2(x, axis):
        hi = x.astype(BF16)
        lo = (x - hi.astype(F32)).astype(BF16)
        return jnp.concatenate([hi, lo], axis=axis)

    def l2n(x):
        return x * lax.rsqrt(jnp.sum(x * x, axis=-1, keepdims=True) + NORM_EPS)

    def prep(gi, carry):
        r0 = pl.multiple_of(gi * rows_g, rows_g)
        rows = pl.ds(r0, rows_g)
        before = pl.ds(pl.multiple_of(jnp.maximum(r0 - c, 0), c), c)
        xg = jnp.concatenate([q_ref[rows, :], k_ref[rows, :], v_ref[rows, :]], axis=1).astype(BF16)
        xp = jnp.concatenate([q_ref[before, :], k_ref[before, :], v_ref[before, :]], axis=1).astype(BF16)
        xp = jnp.where(r0 > 0, xp, jnp.zeros_like(xp))
        ab2 = split2(ab_ref[rows, :], 1)
        agb = jnp.dot(ab2, pick2, preferred_element_type=F32)
        g_b = neg_a * _softplus(agb[:, :LANES] + dtb)
        beta_b = _sigmoid(agb[:, LANES:])
        rng = range(grp)
        sls = [slice(ci * c, (ci + 1) * c) for ci in rng]
        xcs = [xg[s] for s in sls]
        xxs = [jnp.concatenate([xp if ci == 0 else xcs[ci - 1], xcs[ci]], axis=0) for ci in rng]
        shs = [jnp.dot(shift_m, xx, preferred_element_type=F32) for xx in xxs]
        convs = [xc.astype(F32) * cw[CONV_WIDTH - 1:CONV_WIDTH, :]
                 + sh[0:c] * cw[2:3, :] + sh[c:2 * c] * cw[1:2, :] + sh[2 * c:3 * c] * cw[0:1, :]
                 for xc, sh in zip(xcs, shs)]
        acts = [_silu(cv) for cv in convs]
        qcs = [l2n(a[:, :hd]) * (DN_HEAD_DIM ** -0.5) for a in acts]
        kcs = [l2n(a[:, hd:2 * hd]) for a in acts]
        vcs = [a[:, 2 * hd:] for a in acts]
        betas = [beta_b[s] for s in sls]
        gcbs = [jnp.dot(tril2, split2(g_b[s], 0), preferred_element_type=F32) for s in sls]
        gcrs = [gcb.T for gcb in gcbs]
        decays = [jnp.exp(jnp.where(row >= col, gcb - gcr, -jnp.inf)) for gcb, gcr in zip(gcbs, gcrs)]
        kbs = [kc * beta for kc, beta in zip(kcs, betas)]
        lows = [jnp.where(row > col, _bdot_nt(kb, kc) * decay, 0.0) for kb, kc, decay in zip(kbs, kcs, decays)]
        t_invs = _unit_lower_inverse(lows, row, col)
        egs = [jnp.exp(gcb) for gcb in gcbs]
        wus = [_bdot(t_inv, jnp.concatenate([kb * eg, vc * beta], axis=1))
               for t_inv, kb, eg, vc, beta in zip(t_invs, kbs, egs, vcs, betas)]
        attns = [_bdot_nt(qc, kc) * decay for qc, kc, decay in zip(qcs, kcs, decays)]
        gls = [gcb[c - 1:c, :] for gcb in gcbs]
        kes = [kc * jnp.exp(gl - gcb) for kc, gl, gcb in zip(kcs, gls, gcbs)]
        kws = [_bdot_tn(ke, wu) for ke, wu in zip(kes, wus)]
        aws = [_bdot(attn, wu) for attn, wu in zip(attns, wus)]
        for ci in rng:
            rs = pl.ds(pl.multiple_of(r0 + ci * c, c), c)
            sm_s[h, rs, :] = kws[ci][:, :hd].astype(BF16)
            sb_s[h, rs, :] = kws[ci][:, hd:]
            qe_s[h, rs, :] = (qcs[ci] * egs[ci] - aws[ci][:, :hd]).astype(BF16)
            o0_s[h, rs, :] = aws[ci][:, hd:]
            es = pl.ds(pl.multiple_of((gi * grp + ci) * SUBLANES, SUBLANES), SUBLANES)
            egl_s[h, es, :] = jnp.broadcast_to(jnp.exp(gls[ci]), (SUBLANES, LANES))
        return carry

    lax.fori_loop(0, nc // grp, prep, 0)

    @pl.when(h == DN_HEADS - 1)
    def _():
        heads = range(DN_HEADS)

        def step(ci, states):
            rs = pl.ds(pl.multiple_of(ci * c, c), c)
            es = pl.ds(pl.multiple_of(ci * SUBLANES, SUBLANES), SUBLANES)
            sbs = [s.astype(BF16) for s in states]
            for hh in heads:
                st_s[hh, rs, :] = sbs[hh]
            upd = [jnp.dot(sm_s[hh, rs, :], sbs[hh], preferred_element_type=F32) for hh in heads]
            return tuple(states[hh] * egl_s[hh, es, :][0:1, :] - upd[hh] + sb_s[hh, rs, :] for hh in heads)

        lax.fori_loop(0, nc, step, tuple(jnp.zeros((hd, hd), F32) for _ in heads))

        nw = nw_ref[...]
        pgrp = min(DN_POST_GROUP, nc)

        def post(gi, carry):
            r0 = pl.multiple_of(gi * pgrp * c, pgrp * c)
            rss = [pl.ds(pl.multiple_of(r0 + ci * c, c), c) for ci in range(pgrp)]
            for hh in heads:
                lanes = slice(hh * hd, (hh + 1) * hd)
                os_ = [jnp.dot(qe_s[hh, rs, :], st_s[hh, rs, :], preferred_element_type=F32) + o0_s[hh, rs, :]
                       for rs in rss]
                for rs, o in zip(rss, os_):
                    on = o * lax.rsqrt(jnp.mean(o * o, axis=-1, keepdims=True) + NORM_EPS) * nw
                    o_ref[rs, lanes] = (on * _silu(z_ref[rs, lanes].astype(F32))).astype(o_ref.dtype)
            return carry

        lax.fori_loop(0, nc // pgrp, post, 0)


def _deltanet(proj, ab, conv_w, alog_b, dtb_b, norm_w, l, batch, t):
    n = proj.shape[0]
    hb = DN_HEAD_DIM
    qb, kb_, vb_ = OFF_DQ // hb, OFF_DK // hb, OFF_DV // hb
    col = lambda base: pl.BlockSpec((t, hb), lambda b, h: (b, base + h))
    cw = lambda base: _lspec(l, (CONV_WIDTH, hb), lambda b, h: (0, base + h))
    par = _lspec(l, (None, 1, LANES), lambda b, h: (h, 0, 0))
    return pl.pallas_call(
        _dn_kernel,
        grid=(batch, DN_HEADS),
        in_specs=[
            col(qb), col(kb_), col(vb_),
            pl.BlockSpec((t, DN_WIDTH), lambda b, h: (b, OFF_DZ // DN_WIDTH)),
            pl.BlockSpec((t, LANES), lambda b, h: (b, 0)),
            cw(0), cw(DN_HEADS), cw(2 * DN_HEADS),
            par, par,
            _lspec(l, (1, hb), lambda b, h: (0, 0)),
        ],
        out_specs=pl.BlockSpec((t, DN_WIDTH), lambda b, h: (b, 0)),
        out_shape=jax.ShapeDtypeStruct((n, DN_WIDTH), BF16),
        scratch_shapes=[
            pltpu.VMEM((DN_HEADS, t, hb), BF16),
            pltpu.VMEM((DN_HEADS, t, hb), F32),
            pltpu.VMEM((DN_HEADS, t, hb), BF16),
            pltpu.VMEM((DN_HEADS, t, hb), F32),
            pltpu.VMEM((DN_HEADS, t, hb), BF16),
            pltpu.VMEM((DN_HEADS, t // DN_CHUNK * SUBLANES, LANES), F32),
        ],
        compiler_params=_params(("parallel", "arbitrary")),
        name="deltanet",
    )(proj, proj, proj, proj, ab, conv_w, conv_w, conv_w, alog_b, dtb_b, norm_w)


def _rope(x, cos_t, sin_t):
    half = ROPE_DIMS // 2
    lane = lax.broadcasted_iota(jnp.int32, x.shape, 1)
    swapped = jnp.where(lane < half, pltpu.roll(x, LANES - half, axis=1), pltpu.roll(x, half, axis=1))
    return x * cos_t + swapped * sin_t


MB_Q_GROUP = 4


def _moba_kernel(q_ref, k_ref, v_ref, cos_ref, sin_ref, o_ref, kr_s, vt_s, qs_s):
    t = k_ref.shape[0]
    blk = MB_BLOCK
    nb = t // blk
    blocks = [slice(j * blk, (j + 1) * blk) for j in range(nb)]

    kms = []
    for sl in blocks:
        kr = _rope(k_ref[sl, :].astype(F32), cos_ref[sl, :], sin_ref[sl, :])
        kr_s[sl, :] = kr.astype(BF16)
        kms.append(jnp.mean(kr, axis=0, keepdims=True))
        vt_s[:, sl] = v_ref[sl, :].astype(F32).T.astype(BF16)
    km = jnp.concatenate(kms, axis=0)

    bidx = lax.broadcasted_iota(jnp.int32, (nb, blk), 0)
    sels = []
    for i, sl in enumerate(blocks):
        q = _rope(q_ref[sl, :].astype(F32), cos_ref[sl, :], sin_ref[sl, :])
        qs_s[sl, :] = (q * (MB_SCALE * LOG2_E)).astype(BF16)
        g_t = lax.dot_general(km, q, (((1,), (1,)), ((), ())), precision=HIGHEST,
                              preferred_element_type=F32)
        rank = jnp.zeros((nb, blk), F32)
        for m in range(i):
            gm = g_t[m:m + 1, :]
            rank = rank + jnp.where(gm > g_t, 1.0, jnp.where((gm == g_t) & (bidx > m), 1.0, 0.0))
        sels.append(jnp.where((rank < MB_TOP_K) & (bidx < i), 1.0, 0.0))

    krow = lax.broadcasted_iota(jnp.int32, (blk, blk), 0)
    qcol = lax.broadcasted_iota(jnp.int32, (blk, blk), 1)

    for g0 in range(0, nb, MB_Q_GROUP):
        tiles = list(range(g0, min(g0 + MB_Q_GROUP, nb)))
        ss = {}
        for i in tiles:
            for j in range(i + 1):
                s = lax.dot_general(kr_s[blocks[j], :], qs_s[blocks[i], :], (((1,), (1,)), ((), ())),
                                    preferred_element_type=F32)
                mask = (krow <= qcol) if j == i else (sels[i][j:j + 1, :] > 0.5)
                ss[i, j] = jnp.where(mask, s, -jnp.inf)
        m_all = {i: functools.reduce(jnp.maximum, [jnp.max(ss[i, j], axis=0, keepdims=True)
                                                   for j in range(i + 1)]) for i in tiles}
        ps = {(i, j): jnp.exp2(ss[i, j] - m_all[i]) for (i, j) in ss}
        for i in tiles:
            l_all = functools.reduce(jnp.add, [jnp.sum(ps[i, j], axis=0, keepdims=True) for j in range(i + 1)])
            p_all = jnp.concatenate([ps[i, j].astype(BF16) for j in range(i + 1)], axis=0)
            acc = jnp.dot(vt_s[:, :(i + 1) * blk], p_all, preferred_element_type=F32)
            o_ref[blocks[i], :] = (acc / l_all).T.astype(o_ref.dtype)


def _moba(proj, cos_t, sin_t, batch, t):
    n = proj.shape[0]
    hb = MB_HEAD_DIM
    qb, kb_, vb_ = OFF_MQ // hb, OFF_MK // hb, OFF_MV // hb
    full = lambda base: pl.BlockSpec((t, hb), lambda b, h: (b, base + h))
    table = pl.BlockSpec((t, hb), lambda b, h: (0, 0))
    return pl.pallas_call(
        _moba_kernel,
        grid=(batch, MB_HEADS),
        in_specs=[full(qb), full(kb_), full(vb_), table, table],
        out_specs=pl.BlockSpec((t, hb), lambda b, h: (b, h)),
        out_shape=jax.ShapeDtypeStruct((n, MB_WIDTH), BF16),
        scratch_shapes=[
            pltpu.VMEM((t, hb), BF16),
            pltpu.VMEM((hb, t), BF16),
            pltpu.VMEM((t, hb), BF16),
        ],
        compiler_params=_params(("parallel", "parallel")),
        name="moba",
    )(proj, proj, proj, cos_t, sin_t)


LRU_ROWS = 256


def _causal_conv(cur, prev, w):
    ext = jnp.concatenate([prev, cur], axis=0)
    out = cur * w[CONV_WIDTH - 1:CONV_WIDTH, :]
    for s in range(1, CONV_WIDTH):
        shifted = pltpu.roll(ext, s, axis=0)[SUBLANES:, :]
        out = out + shifted * w[CONV_WIDTH - 1 - s:CONV_WIDTH - s, :]
    return out


def _prev_rows(ref, r0):
    start = pl.multiple_of(jnp.maximum(r0 - BF16_ROWS, 0), BF16_ROWS)
    rows = ref[pl.ds(start, BF16_ROWS), :].astype(F32)[BF16_ROWS - SUBLANES:, :]
    return jnp.where(r0 > 0, rows, 0.0)


def _lru_kernel(x_ref, y_ref, cw_ref, cb_ref, wg_ref, bg_ref, lam_ref, o_ref, a_s, u_s):
    t = x_ref.shape[0]
    w = LRU_WIDTH
    rows = min(LRU_ROWS, t)
    neg_c_sp = -LRU_C * _softplus(-lam_ref[...])

    def gates(ci, carry):
        r0 = pl.multiple_of(ci * rows, rows)
        cur = x_ref[pl.ds(r0, rows), :].astype(F32)
        xc = _causal_conv(cur, _prev_rows(x_ref, r0), cw_ref[...]) + cb_ref[...]
        ri = jnp.dot(xc.astype(BF16), wg_ref[...], preferred_element_type=F32) + bg_ref[...]
        r = _sigmoid(ri[:, :w])
        i = _sigmoid(ri[:, w:])
        log_a = r * neg_c_sp
        a = jnp.exp(log_a)
        gain2 = jnp.tanh(-log_a) * (1.0 + a * a)
        a_s[pl.ds(r0, rows), :] = a
        u_s[pl.ds(r0, rows), :] = jnp.sqrt(gain2) * (i * xc)
        return carry

    lax.fori_loop(0, t // rows, gates, 0)

    sub = lax.broadcasted_iota(jnp.int32, (SUBLANES, w), 0)

    def scan(ti, hprev):
        rs = pl.ds(pl.multiple_of(ti * SUBLANES, SUBLANES), SUBLANES)
        a = a_s[rs, :]
        u = u_s[rs, :]
        for s in (1, 2, 4):
            keep = sub >= s
            u = jnp.where(keep, a * pltpu.roll(u, s, axis=0) + u, u)
            a = jnp.where(keep, a * pltpu.roll(a, s, axis=0), a)
        hs = a * hprev + u
        o_ref[rs, :] = (hs * _gelu_tanh(y_ref[rs, :].astype(F32))).astype(o_ref.dtype)
        return hs[SUBLANES - 1:SUBLANES, :]

    lax.fori_loop(0, t // SUBLANES, scan, jnp.zeros((1, w), F32), unroll=4)


def _rglru(proj, conv_w, conv_b, w_gates, b_gates, lam, l, batch, t):
    n = proj.shape[0]
    w = LRU_WIDTH
    const = lambda shape: _lspec(l, shape, lambda b: (0, 0))
    return pl.pallas_call(
        _lru_kernel,
        grid=(batch,),
        in_specs=[
            pl.BlockSpec((t, w), lambda b: (b, OFF_LX // w)),
            pl.BlockSpec((t, w), lambda b: (b, OFF_LY // w)),
            const((CONV_WIDTH, w)), const((1, w)), const((w, 2 * w)), const((1, 2 * w)), const((1, w)),
        ],
        out_specs=pl.BlockSpec((t, w), lambda b: (b, 0)),
        out_shape=jax.ShapeDtypeStruct((n, w), BF16),
        scratch_shapes=[pltpu.VMEM((t, w), F32), pltpu.VMEM((t, w), F32)],
        compiler_params=_params(("parallel",)),
        name="rglru",
    )(proj, proj, conv_w, conv_b, w_gates, b_gates, lam)


FFN_CHUNK = 256


def _mix_ffn_kernel(x_ref, g0_ref, g1_ref, g2_ref, bg_ref, dn_ref, mb_ref, lr_ref,
                    wdn_ref, wmb_ref, wlr_ref, wo_ref, gn_ref, wg_ref, wu_ref, wdown_ref, gf_ref,
                    o_ref, act_s, *, final_norm):
    d = D_MODEL
    bg = bg_ref[...]
    mixed = _sigmoid(g0_ref[...].astype(F32) + bg[:, 0:d]) * jnp.dot(
        dn_ref[...], wdn_ref[...], preferred_element_type=F32)
    mixed = mixed + _sigmoid(g1_ref[...].astype(F32) + bg[:, d:2 * d]) * jnp.dot(
        mb_ref[...], wmb_ref[...], preferred_element_type=F32)
    mixed = mixed + _sigmoid(g2_ref[...].astype(F32) + bg[:, 2 * d:3 * d]) * jnp.dot(
        lr_ref[...], wlr_ref[...], preferred_element_type=F32)
    x = x_ref[...] + jnp.dot(mixed.astype(BF16), wo_ref[...], preferred_element_type=F32)

    ms = jnp.mean(x * x, axis=-1, keepdims=True)
    hn = (x * lax.rsqrt(ms + NORM_EPS) * gn_ref[...]).astype(BF16)
    fc = FFN_CHUNK
    for ci in range(FFN_HIDDEN // fc):
        cols = slice(ci * fc, (ci + 1) * fc)
        gate = jnp.dot(hn, wg_ref[:, cols], preferred_element_type=F32)
        up = jnp.dot(hn, wu_ref[:, cols], preferred_element_type=F32)
        act_s[:, cols] = (_silu(gate) * up).astype(BF16)
    y = x + jnp.dot(act_s[...], wdown_ref[...], preferred_element_type=F32)
    if final_norm:
        ms = jnp.mean(y * y, axis=-1, keepdims=True)
        y = y * lax.rsqrt(ms + NORM_EPS) * gf_ref[...]
    o_ref[...] = y


def _mix_ffn(x2, proj, b_gate, y_dn, y_mb, y_lru, w_dn, w_mb, w_lru, w_o,
             gain, w_in, w_down, gain_final, l, tm, final_norm):
    n, d = x2.shape
    rowblk = lambda width, j: pl.BlockSpec((tm, width), lambda i: (i, j))
    const = lambda shape, j=0: _lspec(l, shape, lambda i: (0, j), single=True)
    g0 = OFF_GATES // d
    return pl.pallas_call(
        functools.partial(_mix_ffn_kernel, final_norm=final_norm),
        grid=(n // tm,),
        in_specs=[
            rowblk(d, 0), rowblk(d, g0), rowblk(d, g0 + 1), rowblk(d, g0 + 2), const((1, N_BRANCH * d)),
            rowblk(DN_WIDTH, 0), rowblk(MB_WIDTH, 0), rowblk(LRU_WIDTH, 0),
            const((DN_WIDTH, d)), const((MB_WIDTH, d)), const((LRU_WIDTH, d)), const((d, d)),
            const((1, d)),
            const((d, FFN_HIDDEN), 0),
            const((d, FFN_HIDDEN), 1),
            const((FFN_HIDDEN, d)),
            pl.BlockSpec((1, d), lambda i: (0, 0)),
        ],
        out_specs=rowblk(d, 0),
        out_shape=jax.ShapeDtypeStruct((n, d), F32),
        scratch_shapes=[pltpu.VMEM((tm, FFN_HIDDEN), BF16)],
        compiler_params=_params(("parallel",)),
        name="mix_ffn",
    )(x2, proj, proj, proj, b_gate, y_dn, y_mb, y_lru, w_dn, w_mb, w_lru, w_o,
      gain, w_in, w_in, w_down, gain_final)


def _rope_tables(t):
    half = ROPE_DIMS // 2
    inv_freq = ROPE_THETA ** (-jnp.arange(half, dtype=F32) / half)
    ang = jnp.arange(t, dtype=F32)[:, None] * inv_freq[None, :]
    cos, sin = jnp.cos(ang), jnp.sin(ang)
    pad = MB_HEAD_DIM - ROPE_DIMS
    cos_t = jnp.concatenate([cos, cos, jnp.ones((t, pad), F32)], axis=1)
    sin_t = jnp.concatenate([-sin, sin, jnp.zeros((t, pad), F32)], axis=1)
    return cos_t, sin_t


def _block_diag(w):
    nl, g, di, do = w.shape
    eye = jnp.eye(g, dtype=w.dtype)
    return (w[:, :, :, None, :] * eye[None, :, None, :, None]).reshape(nl, g * di, g * do)


def _pick_tile(n, pref):
    tile = min(pref, n)
    while n % tile:
        tile //= 2
    return tile


def kernel(x, norm_mix, w_in, b_gate, dn_conv, dn_a_log, dn_dt_bias, dn_norm, dn_out, mb_out, lru_conv_w, lru_conv_b, lru_w_r, lru_b_r, lru_w_i, lru_b_i, lru_lambda, lru_out, w_o, norm_ffn, ffn_in, ffn_down, norm_final):
    batch, t, d = x.shape
    assert d == D_MODEL and t % MB_BLOCK == 0 and t % DN_CHUNK == 0
    n = batch * t
    depth = w_in.shape[0]
    x2 = x.reshape(n, d)
    cos_t, sin_t = _rope_tables(t)
    tm_proj = _pick_tile(n, 1024)
    tm_mix = _pick_tile(n, 512)

    c_ab = 4 * DN_WIDTH
    c_mq = c_ab + 2 * DN_HEADS
    c_g = c_mq + 3 * MB_WIDTH + 2 * LRU_WIDTH

    row3 = lambda p: p[:, None, :]
    w_main = jnp.concatenate([w_in[:, :, c_g:], w_in[:, :, :c_ab], w_in[:, :, c_mq:c_g]], axis=2).astype(BF16)
    w_ab = jnp.pad(w_in[:, :, c_ab:c_mq], ((0, 0), (0, 0), (0, LANES - 2 * DN_HEADS))).astype(BF16)
    alog_b = jnp.broadcast_to(dn_a_log[:, :, None, None], (depth, DN_HEADS, 1, LANES))
    dtb_b = jnp.broadcast_to(dn_dt_bias[:, :, None, None], (depth, DN_HEADS, 1, LANES))
    w_gates = jnp.concatenate([_block_diag(lru_w_r), _block_diag(lru_w_i)], axis=2).astype(BF16)
    b_gates = row3(jnp.concatenate([lru_b_r, lru_b_i], axis=1))
    w_dn, w_mb, w_lru, w_out = (w.astype(BF16) for w in (dn_out, mb_out, lru_out, w_o))
    w_ffn_in = ffn_in.astype(BF16)
    w_ffn_down = ffn_down.astype(BF16)

    for l in range(depth):
        proj, ab = _norm_proj(x2, row3(norm_mix), w_main, w_ab, l, tm_proj, PROJ_TN)
        y_dn = _deltanet(proj, ab, dn_conv, alog_b, dtb_b, row3(dn_norm), l, batch, t)
        y_mb = _moba(proj, cos_t, sin_t, batch, t)
        y_lru = _rglru(proj, lru_conv_w, row3(lru_conv_b), w_gates, b_gates, row3(lru_lambda), l, batch, t)
        x2 = _mix_ffn(x2, proj, row3(b_gate), y_dn, y_mb, y_lru, w_dn, w_mb, w_lru, w_out,
                      row3(norm_ffn), w_ffn_in, w_ffn_down, norm_final[None, :], l, tm_mix,
                      final_norm=(l == depth - 1))

    return x2.reshape(batch, t, d)
```

```python
import functools

import jax
import jax.numpy as jnp
from jax import lax
from jax.experimental import pallas as pl
from jax.experimental.pallas import tpu as pltpu

F32 = jnp.float32
BF16 = jnp.bfloat16
HIGHEST = lax.Precision.HIGHEST

D_MODEL = 1024
NORM_EPS = 1e-6
CONV_WIDTH = 4
DN_HEADS = 4
DN_HEAD_DIM = 128
DN_WIDTH = DN_HEADS * DN_HEAD_DIM
MB_HEADS = 4
MB_HEAD_DIM = 128
MB_WIDTH = MB_HEADS * MB_HEAD_DIM
MB_BLOCK = 256
MB_TOP_K = 3
MB_SCALE = MB_HEAD_DIM ** -0.5
LOG2_E = 1.4426950408889634
ROPE_THETA = 500000.0
ROPE_DIMS = MB_HEAD_DIM // 4
LRU_WIDTH = 512
LRU_GROUPS = 8
LRU_GROUP_DIM = LRU_WIDTH // LRU_GROUPS
LRU_C = 8.0
N_BRANCH = 3
FFN_HIDDEN = -(-8 * D_MODEL // (3 * 256)) * 256

LANES = 128
SUBLANES = 8
BF16_ROWS = 16
VMEM_LIMIT = 56 * 1024 * 1024

OFF_GATES = 0
OFF_DQ = OFF_GATES + N_BRANCH * D_MODEL
OFF_DK = OFF_DQ + DN_WIDTH
OFF_DV = OFF_DK + DN_WIDTH
OFF_DZ = OFF_DV + DN_WIDTH
OFF_MQ = OFF_DZ + DN_WIDTH
OFF_MK = OFF_MQ + MB_WIDTH
OFF_MV = OFF_MK + MB_WIDTH
OFF_LX = OFF_MV + MB_WIDTH
OFF_LY = OFF_LX + LRU_WIDTH
PROJ_COLS = OFF_LY + LRU_WIDTH
PROJ_TN = 2560

DN_CHUNK = 128
DN_INV_BLOCK = 16
DN_GROUP = 16
DN_POST_GROUP = 8


def _lspec(l, shape, index_map, single=False):
    mode = dict(pipeline_mode=pl.Buffered(1)) if single else {}
    return pl.BlockSpec((None,) + shape, lambda *g: (l,) + index_map(*g), **mode)


def _params(sem, **kw):
    return pltpu.CompilerParams(dimension_semantics=sem, vmem_limit_bytes=VMEM_LIMIT, **kw)


def _sigmoid(x):
    return 1.0 / (1.0 + jnp.exp(-x))


def _silu(x):
    return x * _sigmoid(x)


def _softplus(x):
    return jnp.maximum(x, 0.0) + jnp.log1p(jnp.exp(-jnp.abs(x)))


def _gelu_tanh(x):
    c = (2.0 / jnp.pi) ** 0.5
    return 0.5 * x * (1.0 + jnp.tanh(c * (x + 0.044715 * (x * x * x))))


def _bdot(a, b):
    return jnp.dot(a.astype(BF16), b.astype(BF16), preferred_element_type=F32)


def _bdot_nt(a, b):
    return lax.dot_general(a.astype(BF16), b.astype(BF16), (((1,), (1,)), ((), ())),
                           preferred_element_type=F32)


def _bdot_tn(a, b):
    return lax.dot_general(a.astype(BF16), b.astype(BF16), (((0,), (0,)), ((), ())),
                           preferred_element_type=F32)


def _norm_proj_kernel(x_ref, g_ref, w_ref, wab_ref, o_ref, oab_ref, hn_ref):
    @pl.when(pl.program_id(1) == 0)
    def _():
        x = x_ref[...]
        ms = jnp.mean(x * x, axis=-1, keepdims=True)
        hn = (x * lax.rsqrt(ms + NORM_EPS) * g_ref[...]).astype(BF16)
        hn_ref[...] = hn
        oab_ref[...] = jnp.dot(hn, wab_ref[...], preferred_element_type=F32)

    o_ref[...] = jnp.dot(hn_ref[...], w_ref[...], preferred_element_type=F32).astype(o_ref.dtype)


def _norm_proj(x2, gain, w_main, w_ab, l, tm, tn):
    n, d = x2.shape
    cols = w_main.shape[2]
    return pl.pallas_call(
        _norm_proj_kernel,
        grid=(n // tm, cols // tn),
        in_specs=[
            pl.BlockSpec((tm, d), lambda i, j: (i, 0)),
            _lspec(l, (1, d), lambda i, j: (0, 0)),
            _lspec(l, (d, tn), lambda i, j: (0, j)),
            _lspec(l, (d, LANES), lambda i, j: (0, 0)),
        ],
        out_specs=[
            pl.BlockSpec((tm, tn), lambda i, j: (i, j)),
            pl.BlockSpec((tm, LANES), lambda i, j: (i, 0)),
        ],
        out_shape=[
            jax.ShapeDtypeStruct((n, cols), BF16),
            jax.ShapeDtypeStruct((n, LANES), F32),
        ],
        scratch_shapes=[pltpu.VMEM((tm, d), BF16)],
        compiler_params=_params(("parallel", "arbitrary")),
        name="norm_proj",
    )(x2, gain, w_main, w_ab)


def _unit_lower_inverse(lows, row, col):
    c = lows[0].shape[0]
    same_blk = (row // DN_INV_BLOCK) == (col // DN_INV_BLOCK)
    eye = jnp.where(row == col, 1.0, 0.0).astype(F32)

    def neumann(ns, nilpotency):
        invs = [eye - n for n in ns]
        pws = [_bdot(n, n) for n in ns]
        levels = nilpotency.bit_length() - 2
        for lvl in range(levels):
            if lvl < levels - 1:
                both = [_bdot(pw, jnp.concatenate([pw, inv], axis=1)) for pw, inv in zip(pws, invs)]
                pws = [b[:, :c] for b in both]
                invs = [inv + b[:, c:] for inv, b in zip(invs, both)]
            else:
                invs = [inv + _bdot(pw, inv) for pw, inv in zip(pws, invs)]
        return invs

    dgs = [jnp.where(same_blk, low, 0.0) for low in lows]
    offs = [low - dg for low, dg in zip(lows, dgs)]
    inv_ds = neumann(dgs, DN_INV_BLOCK)
    inv_ms = neumann([_bdot(inv_d, off) for inv_d, off in zip(inv_ds, offs)], c // DN_INV_BLOCK)
    return [_bdot(inv_m, inv_d) for inv_m, inv_d in zip(inv_ms, inv_ds)]


def _dn_kernel(q_ref, k_ref, v_ref, z_ref, ab_ref, cwq_ref, cwk_ref, cwv_ref, alog_ref, dtb_ref, nw_ref,
               o_ref, sm_s, sb_s, qe_s, o0_s, st_s, egl_s):
    t = q_ref.shape[0]
    c = DN_CHUNK
    hd = DN_HEAD_DIM
    nc = t // c
    grp = min(DN_GROUP, nc)
    rows_g = grp * c
    h = pl.program_id(1)

    row = lax.broadcasted_iota(jnp.int32, (c, c), 0)
    col = lax.broadcasted_iota(jnp.int32, (c, c), 1)
    r2 = lax.broadcasted_iota(jnp.int32, (c, 2 * c), 0)
    c2 = lax.broadcasted_iota(jnp.int32, (c, 2 * c), 1)
    tril2 = jnp.where(r2 >= c2 % c, 1.0, 0.0).astype(BF16)
    pr = lax.broadcasted_iota(jnp.int32, (2 * LANES, 2 * LANES), 0) % LANES
    pc = lax.broadcasted_iota(jnp.int32, (2 * LANES, 2 * LANES), 1)
    pick2 = jnp.where(pr == jnp.where(pc < LANES, h, h + DN_HEADS), 1.0, 0.0).astype(BF16)
    sr = lax.broadcasted_iota(jnp.int32, ((CONV_WIDTH - 1) * c, 2 * c), 0)
    sc = lax.broadcasted_iota(jnp.int32, ((CONV_WIDTH - 1) * c, 2 * c), 1)
    shift_m = jnp.where(sc == c + sr % c - (sr // c + 1), 1.0, 0.0).astype(BF16)
    neg_a = -jnp.exp(alog_ref[...])
    dtb = dtb_ref[...]
    cw = jnp.concatenate([cwq_ref[...], cwk_ref[...], cwv_ref[...]], axis=1)

    def split2(x, axis):
        hi = x.astype(BF16)
        lo = (x - hi.astype(F32)).astype(BF16)
        return jnp.concatenate([hi, lo], axis=axis)

    def l2n(x):
        return x * lax.rsqrt(jnp.sum(x * x, axis=-1, keepdims=True) + NORM_EPS)

    def prep(gi, carry):
        r0 = pl.multiple_of(gi * rows_g, rows_g)
        rows = pl.ds(r0, rows_g)
        before = pl.ds(pl.multiple_of(jnp.maximum(r0 - c, 0), c), c)
        xg = jnp.concatenate([q_ref[rows, :], k_ref[rows, :], v_ref[rows, :]], axis=1).astype(BF16)
        xp = jnp.concatenate([q_ref[before, :], k_ref[before, :], v_ref[before, :]], axis=1).astype(BF16)
        xp = jnp.where(r0 > 0, xp, jnp.zeros_like(xp))
        ab2 = split2(ab_ref[rows, :], 1)
        agb = jnp.dot(ab2, pick2, preferred_element_type=F32)
        g_b = neg_a * _softplus(agb[:, :LANES] + dtb)
        beta_b = _sigmoid(agb[:, LANES:])
        rng = range(grp)
        sls = [slice(ci * c, (ci + 1) * c) for ci in rng]
        xcs = [xg[s] for s in sls]
        xxs = [jnp.concatenate([xp if ci == 0 else xcs[ci - 1], xcs[ci]], axis=0) for ci in rng]
        shs = [jnp.dot(shift_m, xx, preferred_element_type=F32) for xx in xxs]
        convs = [xc.astype(F32) * cw[CONV_WIDTH - 1:CONV_WIDTH, :]
                 + sh[0:c] * cw[2:3, :] + sh[c:2 * c] * cw[1:2, :] + sh[2 * c:3 * c] * cw[0:1, :]
                 for xc, sh in zip(xcs, shs)]
        acts = [_silu(cv) for cv in convs]
        qcs = [l2n(a[:, :hd]) * (DN_HEAD_DIM ** -0.5) for a in acts]
        kcs = [l2n(a[:, hd:2 * hd]) for a in acts]
        vcs = [a[:, 2 * hd:] for a in acts]
        betas = [beta_b[s] for s in sls]
        gcbs = [jnp.dot(tril2, split2(g_b[s], 0), preferred_element_type=F32) for s in sls]
        gcrs = [gcb.T for gcb in gcbs]
        decays = [jnp.exp(jnp.where(row >= col, gcb - gcr, -jnp.inf)) for gcb, gcr in zip(gcbs, gcrs)]
        kbs = [kc * beta for kc, beta in zip(kcs, betas)]
        lows = [jnp.where(row > col, _bdot_nt(kb, kc) * decay, 0.0) for kb, kc, decay in zip(kbs, kcs, decays)]
        t_invs = _unit_lower_inverse(lows, row, col)
        egs = [jnp.exp(gcb) for gcb in gcbs]
        wus = [_bdot(t_inv, jnp.concatenate([kb * eg, vc * beta], axis=1))
               for t_inv, kb, eg, vc, beta in zip(t_invs, kbs, egs, vcs, betas)]
        attns = [_bdot_nt(qc, kc) * decay for qc, kc, decay in zip(qcs, kcs, decays)]
        gls = [gcb[c - 1:c, :] for gcb in gcbs]
        kes = [kc * jnp.exp(gl - gcb) for kc, gl, gcb in zip(kcs, gls, gcbs)]
        kws = [_bdot_tn(ke, wu) for ke, wu in zip(kes, wus)]
        aws = [_bdot(attn, wu) for attn, wu in zip(attns, wus)]
        for ci in rng:
            rs = pl.ds(pl.multiple_of(r0 + ci * c, c), c)
            sm_s[h, rs, :] = kws[ci][:, :hd].astype(BF16)
            sb_s[h, rs, :] = kws[ci][:, hd:]
            qe_s[h, rs, :] = (qcs[ci] * egs[ci] - aws[ci][:, :hd]).astype(BF16)
            o0_s[h, rs, :] = aws[ci][:, hd:]
            es = pl.ds(pl.multiple_of((gi * grp + ci) * SUBLANES, SUBLANES), SUBLANES)
            egl_s[h, es, :] = jnp.broadcast_to(jnp.exp(gls[ci]), (SUBLANES, LANES))
        return carry

    lax.fori_loop(0, nc // grp, prep, 0)

    @pl.when(h == DN_HEADS - 1)
    def _():
        heads = range(DN_HEADS)

        def step(ci, states):
            rs = pl.ds(pl.multiple_of(ci * c, c), c)
            es = pl.ds(pl.multiple_of(ci * SUBLANES, SUBLANES), SUBLANES)
            sbs = [s.astype(BF16) for s in states]
            for hh in heads:
                st_s[hh, rs, :] = sbs[hh]
            upd = [jnp.dot(sm_s[hh, rs, :], sbs[hh], preferred_element_type=F32) for hh in heads]
            return tuple(states[hh] * egl_s[hh, es, :][0:1, :] - upd[hh] + sb_s[hh, rs, :] for hh in heads)

        lax.fori_loop(0, nc, step, tuple(jnp.zeros((hd, hd), F32) for _ in heads))

        nw = nw_ref[...]
        pgrp = min(DN_POST_GROUP, nc)

        def post(gi, carry):
            r0 = pl.multiple_of(gi * pgrp * c, pgrp * c)
            rss = [pl.ds(pl.multiple_of(r0 + ci * c, c), c) for ci in range(pgrp)]
            for hh in heads:
                lanes = slice(hh * hd, (hh + 1) * hd)
                os_ = [jnp.dot(qe_s[hh, rs, :], st_s[hh, rs, :], preferred_element_type=F32) + o0_s[hh, rs, :]
                       for rs in rss]
                for rs, o in zip(rss, os_):
                    on = o * lax.rsqrt(jnp.mean(o * o, axis=-1, keepdims=True) + NORM_EPS) * nw
                    o_ref[rs, lanes] = (on * _silu(z_ref[rs, lanes].astype(F32))).astype(o_ref.dtype)
            return carry

        lax.fori_loop(0, nc // pgrp, post, 0)


def _deltanet(proj, ab, conv_w, alog_b, dtb_b, norm_w, l, batch, t):
    n = proj.shape[0]
    hb = DN_HEAD_DIM
    qb, kb_, vb_ = OFF_DQ // hb, OFF_DK // hb, OFF_DV // hb
    col = lambda base: pl.BlockSpec((t, hb), lambda b, h: (b, base + h))
    cw = lambda base: _lspec(l, (CONV_WIDTH, hb), lambda b, h: (0, base + h))
    par = _lspec(l, (None, 1, LANES), lambda b, h: (h, 0, 0))
    return pl.pallas_call(
        _dn_kernel,
        grid=(batch, DN_HEADS),
        in_specs=[
            col(qb), col(kb_), col(vb_),
            pl.BlockSpec((t, DN_WIDTH), lambda b, h: (b, OFF_DZ // DN_WIDTH)),
            pl.BlockSpec((t, LANES), lambda b, h: (b, 0)),
            cw(0), cw(DN_HEADS), cw(2 * DN_HEADS),
            par, par,
            _lspec(l, (1, hb), lambda b, h: (0, 0)),
        ],
        out_specs=pl.BlockSpec((t, DN_WIDTH), lambda b, h: (b, 0)),
        out_shape=jax.ShapeDtypeStruct((n, DN_WIDTH), BF16),
        scratch_shapes=[
            pltpu.VMEM((DN_HEADS, t, hb), BF16),
            pltpu.VMEM((DN_HEADS, t, hb), F32),
            pltpu.VMEM((DN_HEADS, t, hb), BF16),
            pltpu.VMEM((DN_HEADS, t, hb), F32),
            pltpu.VMEM((DN_HEADS, t, hb), BF16),
            pltpu.VMEM((DN_HEADS, t // DN_CHUNK * SUBLANES, LANES), F32),
        ],
        compiler_params=_params(("parallel", "arbitrary")),
        name="deltanet",
    )(proj, proj, proj, proj, ab, conv_w, conv_w, conv_w, alog_b, dtb_b, norm_w)


def _rope(x, cos_t, sin_t):
    half = ROPE_DIMS // 2
    lane = lax.broadcasted_iota(jnp.int32, x.shape, 1)
    swapped = jnp.where(lane < half, pltpu.roll(x, LANES - half, axis=1), pltpu.roll(x, half, axis=1))
    return x * cos_t + swapped * sin_t


MB_Q_GROUP = 8


def _moba_kernel(q_ref, k_ref, v_ref, cos_ref, sin_ref, o_ref, kr_s, vt_s, qs_s):
    t = k_ref.shape[0]
    blk = MB_BLOCK
    nb = t // blk
    blocks = [slice(j * blk, (j + 1) * blk) for j in range(nb)]

    kms = []
    for sl in blocks:
        kr = _rope(k_ref[sl, :].astype(F32), cos_ref[sl, :], sin_ref[sl, :])
        kr_s[sl, :] = kr.astype(BF16)
        kms.append(jnp.mean(kr, axis=0, keepdims=True))
        vt_s[:, sl] = v_ref[sl, :].astype(F32).T.astype(BF16)
    km = jnp.concatenate(kms, axis=0)

    bidx = lax.broadcasted_iota(jnp.int32, (nb, blk), 0)
    sels = []
    for i, sl in enumerate(blocks):
        q = _rope(q_ref[sl, :].astype(F32), cos_ref[sl, :], sin_ref[sl, :])
        qs_s[sl, :] = (q * (MB_SCALE * LOG2_E)).astype(BF16)
        g_t = lax.dot_general(km, q, (((1,), (1,)), ((), ())), precision=HIGHEST,
                              preferred_element_type=F32)
        rank = jnp.zeros((nb, blk), F32)
        for m in range(i):
            gm = g_t[m:m + 1, :]
            rank = rank + jnp.where(gm > g_t, 1.0, jnp.where((gm == g_t) & (bidx > m), 1.0, 0.0))
        sels.append(jnp.where((rank < MB_TOP_K) & (bidx < i), 1.0, 0.0))

    krow = lax.broadcasted_iota(jnp.int32, (blk, blk), 0)
    qcol = lax.broadcasted_iota(jnp.int32, (blk, blk), 1)

    for g0 in range(0, nb, MB_Q_GROUP):
        tiles = list(range(g0, min(g0 + MB_Q_GROUP, nb)))
        ss = {}
        for i in tiles:
            for j in range(i + 1):
                s = lax.dot_general(kr_s[blocks[j], :], qs_s[blocks[i], :], (((1,), (1,)), ((), ())),
                                    preferred_element_type=F32)
                mask = (krow <= qcol) if j == i else (sels[i][j:j + 1, :] > 0.5)
                ss[i, j] = jnp.where(mask, s, -jnp.inf)
        m_all = {i: functools.reduce(jnp.maximum, [jnp.max(ss[i, j], axis=0, keepdims=True)
                                                   for j in range(i + 1)]) for i in tiles}
        ps = {(i, j): jnp.exp2(ss[i, j] - m_all[i]) for (i, j) in ss}
        for i in tiles:
            l_all = functools.reduce(jnp.add, [jnp.sum(ps[i, j], axis=0, keepdims=True) for j in range(i + 1)])
            p_all = jnp.concatenate([ps[i, j].astype(BF16) for j in range(i + 1)], axis=0)
            acc = jnp.dot(vt_s[:, :(i + 1) * blk], p_all, preferred_element_type=F32)
            o_ref[blocks[i], :] = (acc / l_all).T.astype(o_ref.dtype)


def _moba(proj, cos_t, sin_t, batch, t):
    n = proj.shape[0]
    hb = MB_HEAD_DIM
    qb, kb_, vb_ = OFF_MQ // hb, OFF_MK // hb, OFF_MV // hb
    full = lambda base: pl.BlockSpec((t, hb), lambda b, h: (b, base + h))
    table = pl.BlockSpec((t, hb), lambda b, h: (0, 0))
    return pl.pallas_call(
        _moba_kernel,
        grid=(batch, MB_HEADS),
        in_specs=[full(qb), full(kb_), full(vb_), table, table],
        out_specs=pl.BlockSpec((t, hb), lambda b, h: (b, h)),
        out_shape=jax.ShapeDtypeStruct((n, MB_WIDTH), BF16),
        scratch_shapes=[
            pltpu.VMEM((t, hb), BF16),
            pltpu.VMEM((hb, t), BF16),
            pltpu.VMEM((t, hb), BF16),
        ],
        compiler_params=_params(("parallel", "parallel")),
        name="moba",
    )(proj, proj, proj, cos_t, sin_t)


def _causal_conv(cur, prev, w):
    ext = jnp.concatenate([prev, cur], axis=0)
    out = cur * w[CONV_WIDTH - 1:CONV_WIDTH, :]
    for s in range(1, CONV_WIDTH):
        shifted = pltpu.roll(ext, s, axis=0)[SUBLANES:, :]
        out = out + shifted * w[CONV_WIDTH - 1 - s:CONV_WIDTH - s, :]
    return out


def _prev_rows(ref, r0):
    start = pl.multiple_of(jnp.maximum(r0 - BF16_ROWS, 0), BF16_ROWS)
    rows = ref[pl.ds(start, BF16_ROWS), :].astype(F32)[BF16_ROWS - SUBLANES:, :]
    return jnp.where(r0 > 0, rows, 0.0)


def _lru_kernel(x_ref, y_ref, cw_ref, cb_ref, wg_ref, bg_ref, lam_ref, o_ref, a_s, u_s):
    t = x_ref.shape[0]
    w = LRU_WIDTH
    seg = SUBLANES
    rows = t // seg
    neg_c_sp = -LRU_C * _softplus(-lam_ref[...])
    lanes = [slice(g * LANES, (g + 1) * LANES) for g in range(w // LANES)]

    def gates(ci, carry):
        r0 = pl.multiple_of(ci * rows, rows)
        cur = x_ref[pl.ds(r0, rows), :].astype(F32)
        xc = _causal_conv(cur, _prev_rows(x_ref, r0), cw_ref[...]) + cb_ref[...]
        ri = jnp.dot(xc.astype(BF16), wg_ref[...], preferred_element_type=F32) + bg_ref[...]
        r = _sigmoid(ri[:, :w])
        i = _sigmoid(ri[:, w:])
        log_a = r * neg_c_sp
        a = jnp.exp(log_a)
        gain2 = jnp.tanh(-log_a) * (1.0 + a * a)
        u = jnp.sqrt(gain2) * (i * xc)
        for g, ln in enumerate(lanes):
            a_s[g, pl.ds(ci, rows, stride=seg), :] = a[:, ln]
            u_s[g, pl.ds(ci, rows, stride=seg), :] = u[:, ln]
        return carry

    lax.fori_loop(0, seg, gates, 0)

    def local(r, carry):
        h, p = carry
        rs = pl.ds(pl.multiple_of(r * seg, seg), seg)
        a = jnp.concatenate([a_s[g, rs, :] for g in range(len(lanes))], axis=1)
        h = a * h + jnp.concatenate([u_s[g, rs, :] for g in range(len(lanes))], axis=1)
        p = a * p
        for g, ln in enumerate(lanes):
            u_s[g, rs, :] = h[:, ln]
            a_s[g, rs, :] = p[:, ln]
        return h, p

    h_fin, p_fin = lax.fori_loop(0, rows, local, (jnp.zeros((seg, w), F32), jnp.ones((seg, w), F32)),
                                 unroll=4)

    state = jnp.zeros((1, w), F32)
    incoming = []
    for s in range(seg):
        incoming.append(state)
        state = h_fin[s:s + 1, :] + p_fin[s:s + 1, :] * state

    for s in range(seg):
        def finish(rb, carry, s=s):
            src = pl.ds(rb * (seg * seg) + s, seg, stride=seg)
            dst = pl.ds(pl.multiple_of(s * rows + rb * seg, seg), seg)
            hl = jnp.concatenate([u_s[g, src, :] for g in range(len(lanes))], axis=1)
            pr = jnp.concatenate([a_s[g, src, :] for g in range(len(lanes))], axis=1)
            hs = hl + pr * incoming[s]
            o_ref[dst, :] = (hs * _gelu_tanh(y_ref[dst, :].astype(F32))).astype(o_ref.dtype)
            return carry

        lax.fori_loop(0, rows // seg, finish, 0, unroll=4)


def _rglru(proj, conv_w, conv_b, w_gates, b_gates, lam, l, batch, t):
    n = proj.shape[0]
    w = LRU_WIDTH
    const = lambda shape: _lspec(l, shape, lambda b: (0, 0))
    return pl.pallas_call(
        _lru_kernel,
        grid=(batch,),
        in_specs=[
            pl.BlockSpec((t, w), lambda b: (b, OFF_LX // w)),
            pl.BlockSpec((t, w), lambda b: (b, OFF_LY // w)),
            const((CONV_WIDTH, w)), const((1, w)), const((w, 2 * w)), const((1, 2 * w)), const((1, w)),
        ],
        out_specs=pl.BlockSpec((t, w), lambda b: (b, 0)),
        out_shape=jax.ShapeDtypeStruct((n, w), BF16),
        scratch_shapes=[pltpu.VMEM((w // LANES, t, LANES), F32), pltpu.VMEM((w // LANES, t, LANES), F32)],
        compiler_params=_params(("parallel",)),
        name="rglru",
    )(proj, proj, conv_w, conv_b, w_gates, b_gates, lam)


FFN_CHUNK = 256


def _mix_ffn_kernel(x_ref, g0_ref, g1_ref, g2_ref, bg_ref, dn_ref, mb_ref, lr_ref,
                    wdn_ref, wmb_ref, wlr_ref, wo_ref, gn_ref, wg_ref, wu_ref, wdown_ref, gf_ref,
                    o_ref, act_s, *, final_norm):
    d = D_MODEL
    bg = bg_ref[...]
    mixed = _sigmoid(g0_ref[...].astype(F32) + bg[:, 0:d]) * jnp.dot(
        dn_ref[...], wdn_ref[...], preferred_element_type=F32)
    mixed = mixed + _sigmoid(g1_ref[...].astype(F32) + bg[:, d:2 * d]) * jnp.dot(
        mb_ref[...], wmb_ref[...], preferred_element_type=F32)
    mixed = mixed + _sigmoid(g2_ref[...].astype(F32) + bg[:, 2 * d:3 * d]) * jnp.dot(
        lr_ref[...], wlr_ref[...], preferred_element_type=F32)
    x = x_ref[...] + jnp.dot(mixed.astype(BF16), wo_ref[...], preferred_element_type=F32)

    ms = jnp.mean(x * x, axis=-1, keepdims=True)
    hn = (x * lax.rsqrt(ms + NORM_EPS) * gn_ref[...]).astype(BF16)
    fc = FFN_CHUNK
    for ci in range(FFN_HIDDEN // fc):
        cols = slice(ci * fc, (ci + 1) * fc)
        gate = jnp.dot(hn, wg_ref[:, cols], preferred_element_type=F32)
        up = jnp.dot(hn, wu_ref[:, cols], preferred_element_type=F32)
        act_s[:, cols] = (_silu(gate) * up).astype(BF16)
    y = x + jnp.dot(act_s[...], wdown_ref[...], preferred_element_type=F32)
    if final_norm:
        ms = jnp.mean(y * y, axis=-1, keepdims=True)
        y = y * lax.rsqrt(ms + NORM_EPS) * gf_ref[...]
    o_ref[...] = y


def _mix_ffn(x2, proj, b_gate, y_dn, y_mb, y_lru, w_dn, w_mb, w_lru, w_o,
             gain, w_in, w_down, gain_final, l, tm, final_norm):
    n, d = x2.shape
    rowblk = lambda width, j: pl.BlockSpec((tm, width), lambda i: (i, j))
    const = lambda shape, j=0: _lspec(l, shape, lambda i: (0, j), single=True)
    g0 = OFF_GATES // d
    return pl.pallas_call(
        functools.partial(_mix_ffn_kernel, final_norm=final_norm),
        grid=(n // tm,),
        in_specs=[
            rowblk(d, 0), rowblk(d, g0), rowblk(d, g0 + 1), rowblk(d, g0 + 2), const((1, N_BRANCH * d)),
            rowblk(DN_WIDTH, 0), rowblk(MB_WIDTH, 0), rowblk(LRU_WIDTH, 0),
            const((DN_WIDTH, d)), const((MB_WIDTH, d)), const((LRU_WIDTH, d)), const((d, d)),
            const((1, d)),
            const((d, FFN_HIDDEN), 0),
            const((d, FFN_HIDDEN), 1),
            const((FFN_HIDDEN, d)),
            pl.BlockSpec((1, d), lambda i: (0, 0)),
        ],
        out_specs=rowblk(d, 0),
        out_shape=jax.ShapeDtypeStruct((n, d), F32),
        scratch_shapes=[pltpu.VMEM((tm, FFN_HIDDEN), BF16)],
        compiler_params=_params(("parallel",)),
        name="mix_ffn",
    )(x2, proj, proj, proj, b_gate, y_dn, y_mb, y_lru, w_dn, w_mb, w_lru, w_o,
      gain, w_in, w_in, w_down, gain_final)


def _rope_tables(t):
    half = ROPE_DIMS // 2
    inv_freq = ROPE_THETA ** (-jnp.arange(half, dtype=F32) / half)
    ang = jnp.arange(t, dtype=F32)[:, None] * inv_freq[None, :]
    cos, sin = jnp.cos(ang), jnp.sin(ang)
    pad = MB_HEAD_DIM - ROPE_DIMS
    cos_t = jnp.concatenate([cos, cos, jnp.ones((t, pad), F32)], axis=1)
    sin_t = jnp.concatenate([-sin, sin, jnp.zeros((t, pad), F32)], axis=1)
    return cos_t, sin_t


def _block_diag(w):
    nl, g, di, do = w.shape
    eye = jnp.eye(g, dtype=w.dtype)
    return (w[:, :, :, None, :] * eye[None, :, None, :, None]).reshape(nl, g * di, g * do)


def _pick_tile(n, pref):
    tile = min(pref, n)
    while n % tile:
        tile //= 2
    return tile


def kernel(x, norm_mix, w_in, b_gate, dn_conv, dn_a_log, dn_dt_bias, dn_norm, dn_out, mb_out, lru_conv_w, lru_conv_b, lru_w_r, lru_b_r, lru_w_i, lru_b_i, lru_lambda, lru_out, w_o, norm_ffn, ffn_in, ffn_down, norm_final):
    batch, t, d = x.shape
    assert d == D_MODEL and t % MB_BLOCK == 0 and t % DN_CHUNK == 0
    n = batch * t
    depth = w_in.shape[0]
    x2 = x.reshape(n, d)
    cos_t, sin_t = _rope_tables(t)
    tm_proj = _pick_tile(n, 1024)
    tm_mix = _pick_tile(n, 512)

    c_ab = 4 * DN_WIDTH
    c_mq = c_ab + 2 * DN_HEADS
    c_g = c_mq + 3 * MB_WIDTH + 2 * LRU_WIDTH

    row3 = lambda p: p[:, None, :]
    w_main = jnp.concatenate([w_in[:, :, c_g:], w_in[:, :, :c_ab], w_in[:, :, c_mq:c_g]], axis=2).astype(BF16)
    w_ab = jnp.pad(w_in[:, :, c_ab:c_mq], ((0, 0), (0, 0), (0, LANES - 2 * DN_HEADS))).astype(BF16)
    alog_b = jnp.broadcast_to(dn_a_log[:, :, None, None], (depth, DN_HEADS, 1, LANES))
    dtb_b = jnp.broadcast_to(dn_dt_bias[:, :, None, None], (depth, DN_HEADS, 1, LANES))
    w_gates = jnp.concatenate([_block_diag(lru_w_r), _block_diag(lru_w_i)], axis=2).astype(BF16)
    b_gates = row3(jnp.concatenate([lru_b_r, lru_b_i], axis=1))
    w_dn, w_mb, w_lru, w_out = (w.astype(BF16) for w in (dn_out, mb_out, lru_out, w_o))
    w_ffn_in = ffn_in.astype(BF16)
    w_ffn_down = ffn_down.astype(BF16)

    for l in range(depth):
        proj, ab = _norm_proj(x2, row3(norm_mix), w_main, w_ab, l, tm_proj, PROJ_TN)
        y_dn = _deltanet(proj, ab, dn_conv, alog_b, dtb_b, row3(dn_norm), l, batch, t)
        y_mb = _moba(proj, cos_t, sin_t, batch, t)
        y_lru = _rglru(proj, lru_conv_w, row3(lru_conv_b), w_gates, b_gates, row3(lru_lambda), l, batch, t)
        x2 = _mix_ffn(x2, proj, row3(b_gate), y_dn, y_mb, y_lru, w_dn, w_mb, w_lru, w_out,
                      row3(norm_ffn), w_ffn_in, w_ffn_down, norm_final[None, :], l, tm_mix,
                      final_norm=(l == depth - 1))

    return x2.reshape(batch, t, d)
```

```python
import functools

import jax
import jax.numpy as jnp
from jax import lax
from jax.experimental import pallas as pl
from jax.experimental.pallas import tpu as pltpu

F32 = jnp.float32
BF16 = jnp.bfloat16
HIGHEST = lax.Precision.HIGHEST

D_MODEL = 1024
NORM_EPS = 1e-6
CONV_WIDTH = 4
DN_HEADS = 4
DN_HEAD_DIM = 128
DN_WIDTH = DN_HEADS * DN_HEAD_DIM
MB_HEADS = 4
MB_HEAD_DIM = 128
MB_WIDTH = MB_HEADS * MB_HEAD_DIM
MB_BLOCK = 256
MB_TOP_K = 3
MB_SCALE = MB_HEAD_DIM ** -0.5
LOG2_E = 1.4426950408889634
ROPE_THETA = 500000.0
ROPE_DIMS = MB_HEAD_DIM // 4
LRU_WIDTH = 512
LRU_GROUPS = 8
LRU_GROUP_DIM = LRU_WIDTH // LRU_GROUPS
LRU_C = 8.0
N_BRANCH = 3
FFN_HIDDEN = -(-8 * D_MODEL // (3 * 256)) * 256

LANES = 128
SUBLANES = 8
BF16_ROWS = 16
VMEM_LIMIT = 56 * 1024 * 1024

OFF_GATES = 0
OFF_DQ = OFF_GATES + N_BRANCH * D_MODEL
OFF_DK = OFF_DQ + DN_WIDTH
OFF_DV = OFF_DK + DN_WIDTH
OFF_DZ = OFF_DV + DN_WIDTH
OFF_MQ = OFF_DZ + DN_WIDTH
OFF_MK = OFF_MQ + MB_WIDTH
OFF_MV = OFF_MK + MB_WIDTH
OFF_LX = OFF_MV + MB_WIDTH
OFF_LY = OFF_LX + LRU_WIDTH
PROJ_COLS = OFF_LY + LRU_WIDTH
PROJ_TN = 2560

DN_CHUNK = 128
DN_INV_BLOCK = 16
DN_GROUP = 16
DN_POST_GROUP = 8


def _lspec(l, shape, index_map, single=False):
    mode = dict(pipeline_mode=pl.Buffered(1)) if single else {}
    return pl.BlockSpec((None,) + shape, lambda *g: (l,) + index_map(*g), **mode)


def _params(sem):
    return pltpu.CompilerParams(dimension_semantics=sem, vmem_limit_bytes=VMEM_LIMIT)


def _sigmoid(x):
    return 1.0 / (1.0 + jnp.exp(-x))


def _silu(x):
    return x * _sigmoid(x)


def _softplus(x):
    return jnp.maximum(x, 0.0) + jnp.log1p(jnp.exp(-jnp.abs(x)))


def _gelu_tanh(x):
    c = (2.0 / jnp.pi) ** 0.5
    return 0.5 * x * (1.0 + jnp.tanh(c * (x + 0.044715 * (x * x * x))))


def _bdot(a, b):
    return jnp.dot(a.astype(BF16), b.astype(BF16), preferred_element_type=F32)


def _bdot_nt(a, b):
    return lax.dot_general(a.astype(BF16), b.astype(BF16), (((1,), (1,)), ((), ())),
                           preferred_element_type=F32)


def _bdot_tn(a, b):
    return lax.dot_general(a.astype(BF16), b.astype(BF16), (((0,), (0,)), ((), ())),
                           preferred_element_type=F32)


def _norm_proj_kernel(x_ref, g_ref, w_ref, wab_ref, o_ref, oab_ref, hn_ref):
    @pl.when(pl.program_id(1) == 0)
    def _():
        x = x_ref[...]
        ms = jnp.mean(x * x, axis=-1, keepdims=True)
        hn = (x * lax.rsqrt(ms + NORM_EPS) * g_ref[...]).astype(BF16)
        hn_ref[...] = hn
        oab_ref[...] = jnp.dot(hn, wab_ref[...], preferred_element_type=F32)

    o_ref[...] = jnp.dot(hn_ref[...], w_ref[...], preferred_element_type=F32).astype(o_ref.dtype)


def _norm_proj(x2, gain, w_main, w_ab, l, tm, tn):
    n, d = x2.shape
    cols = w_main.shape[2]
    return pl.pallas_call(
        _norm_proj_kernel,
        grid=(n // tm, cols // tn),
        in_specs=[
            pl.BlockSpec((tm, d), lambda i, j: (i, 0)),
            _lspec(l, (1, d), lambda i, j: (0, 0)),
            _lspec(l, (d, tn), lambda i, j: (0, j)),
            _lspec(l, (d, LANES), lambda i, j: (0, 0)),
        ],
        out_specs=[
            pl.BlockSpec((tm, tn), lambda i, j: (i, j)),
            pl.BlockSpec((tm, LANES), lambda i, j: (i, 0)),
        ],
        out_shape=[
            jax.ShapeDtypeStruct((n, cols), BF16),
            jax.ShapeDtypeStruct((n, LANES), F32),
        ],
        scratch_shapes=[pltpu.VMEM((tm, d), BF16)],
        compiler_params=_params(("parallel", "arbitrary")),
        name="norm_proj",
    )(x2, gain, w_main, w_ab)


def _unit_lower_inverse(lows, row, col):
    c = lows[0].shape[0]
    same_blk = (row // DN_INV_BLOCK) == (col // DN_INV_BLOCK)
    eye = jnp.where(row == col, 1.0, 0.0).astype(F32)

    def neumann(ns, nilpotency):
        invs = [eye - n for n in ns]
        pws = [_bdot(n, n) for n in ns]
        levels = nilpotency.bit_length() - 2
        for lvl in range(levels):
            if lvl < levels - 1:
                both = [_bdot(pw, jnp.concatenate([pw, inv], axis=1)) for pw, inv in zip(pws, invs)]
                pws = [b[:, :c] for b in both]
                invs = [inv + b[:, c:] for inv, b in zip(invs, both)]
            else:
                invs = [inv + _bdot(pw, inv) for pw, inv in zip(pws, invs)]
        return invs

    dgs = [jnp.where(same_blk, low, 0.0) for low in lows]
    offs = [low - dg for low, dg in zip(lows, dgs)]
    inv_ds = neumann(dgs, DN_INV_BLOCK)
    inv_ms = neumann([_bdot(inv_d, off) for inv_d, off in zip(inv_ds, offs)], c // DN_INV_BLOCK)
    return [_bdot(inv_m, inv_d) for inv_m, inv_d in zip(inv_ms, inv_ds)]


def _dn_kernel(q_ref, k_ref, v_ref, z_ref, ab_ref, cwq_ref, cwk_ref, cwv_ref, alog_ref, dtb_ref, nw_ref,
               o_ref, sm_s, sb_s, qe_s, o0_s, st_s, egl_s):
    t = q_ref.shape[0]
    c = DN_CHUNK
    hd = DN_HEAD_DIM
    nc = t // c
    grp = min(DN_GROUP, nc)
    rows_g = grp * c
    h = pl.program_id(1)

    row = lax.broadcasted_iota(jnp.int32, (c, c), 0)
    col = lax.broadcasted_iota(jnp.int32, (c, c), 1)
    r2 = lax.broadcasted_iota(jnp.int32, (c, 2 * c), 0)
    c2 = lax.broadcasted_iota(jnp.int32, (c, 2 * c), 1)
    tril2 = jnp.where(r2 >= c2 % c, 1.0, 0.0).astype(BF16)
    pr = lax.broadcasted_iota(jnp.int32, (2 * LANES, 2 * LANES), 0) % LANES
    pc = lax.broadcasted_iota(jnp.int32, (2 * LANES, 2 * LANES), 1)
    pick2 = jnp.where(pr == jnp.where(pc < LANES, h, h + DN_HEADS), 1.0, 0.0).astype(BF16)
    sr = lax.broadcasted_iota(jnp.int32, ((CONV_WIDTH - 1) * c, 2 * c), 0)
    sc = lax.broadcasted_iota(jnp.int32, ((CONV_WIDTH - 1) * c, 2 * c), 1)
    shift_m = jnp.where(sc == c + sr % c - (sr // c + 1), 1.0, 0.0).astype(BF16)
    neg_a = -jnp.exp(alog_ref[...])
    dtb = dtb_ref[...]
    cw = jnp.concatenate([cwq_ref[...], cwk_ref[...], cwv_ref[...]], axis=1)

    def split2(x, axis):
        hi = x.astype(BF16)
        lo = (x - hi.astype(F32)).astype(BF16)
        return jnp.concatenate([hi, lo], axis=axis)

    def l2n(x):
        return x * lax.rsqrt(jnp.sum(x * x, axis=-1, keepdims=True) + NORM_EPS)

    def prep(gi, carry):
        r0 = pl.multiple_of(gi * rows_g, rows_g)
        rows = pl.ds(r0, rows_g)
        before = pl.ds(pl.multiple_of(jnp.maximum(r0 - c, 0), c), c)
        xg = jnp.concatenate([q_ref[rows, :], k_ref[rows, :], v_ref[rows, :]], axis=1).astype(BF16)
        xp = jnp.concatenate([q_ref[before, :], k_ref[before, :], v_ref[before, :]], axis=1).astype(BF16)
        xp = jnp.where(r0 > 0, xp, jnp.zeros_like(xp))
        ab2 = split2(ab_ref[rows, :], 1)
        agb = jnp.dot(ab2, pick2, preferred_element_type=F32)
        g_b = neg_a * _softplus(agb[:, :LANES] + dtb)
        beta_b = _sigmoid(agb[:, LANES:])
        rng = range(grp)
        sls = [slice(ci * c, (ci + 1) * c) for ci in rng]
        xcs = [xg[s] for s in sls]
        xxs = [jnp.concatenate([xp if ci == 0 else xcs[ci - 1], xcs[ci]], axis=0) for ci in rng]
        shs = [jnp.dot(shift_m, xx, preferred_element_type=F32) for xx in xxs]
        last = CONV_WIDTH - 1
        convs = [xc.astype(F32) * cw[last:, :]
                 + sum(sh[(k - 1) * c:k * c] * cw[last - k:CONV_WIDTH - k, :] for k in range(1, CONV_WIDTH))
                 for xc, sh in zip(xcs, shs)]
        acts = [_silu(cv) for cv in convs]
        qcs = [l2n(a[:, :hd]) * (DN_HEAD_DIM ** -0.5) for a in acts]
        kcs = [l2n(a[:, hd:2 * hd]) for a in acts]
        vcs = [a[:, 2 * hd:] for a in acts]
        betas = [beta_b[s] for s in sls]
        gcbs = [jnp.dot(tril2, split2(g_b[s], 0), preferred_element_type=F32) for s in sls]
        gcrs = [gcb.T for gcb in gcbs]
        decays = [jnp.exp(jnp.where(row >= col, gcb - gcr, -jnp.inf)) for gcb, gcr in zip(gcbs, gcrs)]
        kbs = [kc * beta for kc, beta in zip(kcs, betas)]
        lows = [jnp.where(row > col, _bdot_nt(kb, kc) * decay, 0.0) for kb, kc, decay in zip(kbs, kcs, decays)]
        t_invs = _unit_lower_inverse(lows, row, col)
        egs = [jnp.exp(gcb) for gcb in gcbs]
        wus = [_bdot(t_inv, jnp.concatenate([kb * eg, vc * beta], axis=1))
               for t_inv, kb, eg, vc, beta in zip(t_invs, kbs, egs, vcs, betas)]
        attns = [_bdot_nt(qc, kc) * decay for qc, kc, decay in zip(qcs, kcs, decays)]
        gls = [gcb[c - 1:c, :] for gcb in gcbs]
        kes = [kc * jnp.exp(gl - gcb) for kc, gl, gcb in zip(kcs, gls, gcbs)]
        kws = [_bdot_tn(ke, wu) for ke, wu in zip(kes, wus)]
        aws = [_bdot(attn, wu) for attn, wu in zip(attns, wus)]
        for ci in rng:
            rs = pl.ds(pl.multiple_of(r0 + ci * c, c), c)
            sm_s[h, rs, :] = kws[ci][:, :hd].astype(BF16)
            sb_s[h, rs, :] = kws[ci][:, hd:]
            qe_s[h, rs, :] = (qcs[ci] * egs[ci] - aws[ci][:, :hd]).astype(BF16)
            o0_s[h, rs, :] = aws[ci][:, hd:]
            es = pl.ds(pl.multiple_of((gi * grp + ci) * SUBLANES, SUBLANES), SUBLANES)
            egl_s[h, es, :] = jnp.broadcast_to(jnp.exp(gls[ci]), (SUBLANES, LANES))
        return carry

    lax.fori_loop(0, nc // grp, prep, 0)

    @pl.when(h == DN_HEADS - 1)
    def _():
        heads = range(DN_HEADS)

        def step(ci, states):
            rs = pl.ds(pl.multiple_of(ci * c, c), c)
            es = pl.ds(pl.multiple_of(ci * SUBLANES, SUBLANES), SUBLANES)
            sbs = [s.astype(BF16) for s in states]
            for hh in heads:
                st_s[hh, rs, :] = sbs[hh]
            upd = [jnp.dot(sm_s[hh, rs, :], sbs[hh], preferred_element_type=F32) for hh in heads]
            return tuple(states[hh] * egl_s[hh, es, :][0:1, :] - upd[hh] + sb_s[hh, rs, :] for hh in heads)

        lax.fori_loop(0, nc, step, tuple(jnp.zeros((hd, hd), F32) for _ in heads))

        nw = nw_ref[...]
        pgrp = min(DN_POST_GROUP, nc)

        def post(gi, carry):
            r0 = pl.multiple_of(gi * pgrp * c, pgrp * c)
            rss = [pl.ds(pl.multiple_of(r0 + ci * c, c), c) for ci in range(pgrp)]
            for hh in heads:
                lanes = slice(hh * hd, (hh + 1) * hd)
                os_ = [jnp.dot(qe_s[hh, rs, :], st_s[hh, rs, :], preferred_element_type=F32) + o0_s[hh, rs, :]
                       for rs in rss]
                for rs, o in zip(rss, os_):
                    on = o * lax.rsqrt(jnp.mean(o * o, axis=-1, keepdims=True) + NORM_EPS) * nw
                    o_ref[rs, lanes] = (on * _silu(z_ref[rs, lanes].astype(F32))).astype(o_ref.dtype)
            return carry

        lax.fori_loop(0, nc // pgrp, post, 0)


def _deltanet(proj, ab, conv_w, alog_b, dtb_b, norm_w, l, batch, t):
    n = proj.shape[0]
    hb = DN_HEAD_DIM
    qb, kb_, vb_ = OFF_DQ // hb, OFF_DK // hb, OFF_DV // hb
    col = lambda base: pl.BlockSpec((t, hb), lambda b, h: (b, base + h))
    cw = lambda base: _lspec(l, (CONV_WIDTH, hb), lambda b, h: (0, base + h))
    par = _lspec(l, (None, 1, LANES), lambda b, h: (h, 0, 0))
    return pl.pallas_call(
        _dn_kernel,
        grid=(batch, DN_HEADS),
        in_specs=[
            col(qb), col(kb_), col(vb_),
            pl.BlockSpec((t, DN_WIDTH), lambda b, h: (b, OFF_DZ // DN_WIDTH)),
            pl.BlockSpec((t, LANES), lambda b, h: (b, 0)),
            cw(0), cw(DN_HEADS), cw(2 * DN_HEADS),
            par, par,
            _lspec(l, (1, hb), lambda b, h: (0, 0)),
        ],
        out_specs=pl.BlockSpec((t, DN_WIDTH), lambda b, h: (b, 0)),
        out_shape=jax.ShapeDtypeStruct((n, DN_WIDTH), BF16),
        scratch_shapes=[
            pltpu.VMEM((DN_HEADS, t, hb), BF16),
            pltpu.VMEM((DN_HEADS, t, hb), F32),
            pltpu.VMEM((DN_HEADS, t, hb), BF16),
            pltpu.VMEM((DN_HEADS, t, hb), F32),
            pltpu.VMEM((DN_HEADS, t, hb), BF16),
            pltpu.VMEM((DN_HEADS, t // DN_CHUNK * SUBLANES, LANES), F32),
        ],
        compiler_params=_params(("parallel", "arbitrary")),
        name="deltanet",
    )(proj, proj, proj, proj, ab, conv_w, conv_w, conv_w, alog_b, dtb_b, norm_w)


def _rope(x, cos_t, sin_t):
    half = ROPE_DIMS // 2
    lane = lax.broadcasted_iota(jnp.int32, x.shape, 1)
    swapped = jnp.where(lane < half, pltpu.roll(x, LANES - half, axis=1), pltpu.roll(x, half, axis=1))
    return x * cos_t + swapped * sin_t


MB_Q_GROUP = 8


def _moba_kernel(q_ref, k_ref, v_ref, cos_ref, sin_ref, o_ref, kr_s, vt_s, qs_s):
    t = k_ref.shape[0]
    blk = MB_BLOCK
    nb = t // blk
    blocks = [slice(j * blk, (j + 1) * blk) for j in range(nb)]

    kms = []
    for sl in blocks:
        kr = _rope(k_ref[sl, :].astype(F32), cos_ref[sl, :], sin_ref[sl, :])
        kr_s[sl, :] = kr.astype(BF16)
        kms.append(jnp.mean(kr, axis=0, keepdims=True))
        vt_s[:, sl] = v_ref[sl, :].astype(F32).T.astype(BF16)
    km = jnp.concatenate(kms, axis=0)

    bidx = lax.broadcasted_iota(jnp.int32, (nb, blk), 0)
    sels = []
    for i, sl in enumerate(blocks):
        q = _rope(q_ref[sl, :].astype(F32), cos_ref[sl, :], sin_ref[sl, :])
        qs_s[sl, :] = (q * (MB_SCALE * LOG2_E)).astype(BF16)
        g_t = lax.dot_general(km, q, (((1,), (1,)), ((), ())), precision=HIGHEST,
                              preferred_element_type=F32)
        rank = jnp.zeros((nb, blk), F32)
        for m in range(i):
            gm = g_t[m:m + 1, :]
            rank = rank + jnp.where(gm > g_t, 1.0, jnp.where((gm == g_t) & (bidx > m), 1.0, 0.0))
        sels.append(jnp.where((rank < MB_TOP_K) & (bidx < i), 1.0, 0.0))

    krow = lax.broadcasted_iota(jnp.int32, (blk, blk), 0)
    qcol = lax.broadcasted_iota(jnp.int32, (blk, blk), 1)

    for g0 in range(0, nb, MB_Q_GROUP):
        tiles = list(range(g0, min(g0 + MB_Q_GROUP, nb)))
        ss = {}
        for i in tiles:
            for j in range(i + 1):
                s = lax.dot_general(kr_s[blocks[j], :], qs_s[blocks[i], :], (((1,), (1,)), ((), ())),
                                    preferred_element_type=F32)
                mask = (krow <= qcol) if j == i else (sels[i][j:j + 1, :] > 0.5)
                ss[i, j] = jnp.where(mask, s, -jnp.inf)
        m_all = {i: functools.reduce(jnp.maximum, [jnp.max(ss[i, j], axis=0, keepdims=True)
                                                   for j in range(i + 1)]) for i in tiles}
        ps = {(i, j): jnp.exp2(ss[i, j] - m_all[i]) for (i, j) in ss}
        for i in tiles:
            l_all = functools.reduce(jnp.add, [jnp.sum(ps[i, j], axis=0, keepdims=True) for j in range(i + 1)])
            p_all = jnp.concatenate([ps[i, j].astype(BF16) for j in range(i + 1)], axis=0)
            acc = jnp.dot(vt_s[:, :(i + 1) * blk], p_all, preferred_element_type=F32)
            o_ref[blocks[i], :] = (acc / l_all).T.astype(o_ref.dtype)


def _moba(proj, cos_t, sin_t, batch, t):
    n = proj.shape[0]
    hb = MB_HEAD_DIM
    qb, kb_, vb_ = OFF_MQ // hb, OFF_MK // hb, OFF_MV // hb
    full = lambda base: pl.BlockSpec((t, hb), lambda b, h: (b, base + h))
    table = pl.BlockSpec((t, hb), lambda b, h: (0, 0))
    return pl.pallas_call(
        _moba_kernel,
        grid=(batch, MB_HEADS),
        in_specs=[full(qb), full(kb_), full(vb_), table, table],
        out_specs=pl.BlockSpec((t, hb), lambda b, h: (b, h)),
        out_shape=jax.ShapeDtypeStruct((n, MB_WIDTH), BF16),
        scratch_shapes=[
            pltpu.VMEM((t, hb), BF16),
            pltpu.VMEM((hb, t), BF16),
            pltpu.VMEM((t, hb), BF16),
        ],
        compiler_params=_params(("parallel", "parallel")),
        name="moba",
    )(proj, proj, proj, cos_t, sin_t)


def _causal_conv(cur, prev, w):
    ext = jnp.concatenate([prev, cur], axis=0)
    out = cur * w[CONV_WIDTH - 1:CONV_WIDTH, :]
    for s in range(1, CONV_WIDTH):
        shifted = pltpu.roll(ext, s, axis=0)[SUBLANES:, :]
        out = out + shifted * w[CONV_WIDTH - 1 - s:CONV_WIDTH - s, :]
    return out


def _prev_rows(ref, r0):
    start = pl.multiple_of(jnp.maximum(r0 - BF16_ROWS, 0), BF16_ROWS)
    rows = ref[pl.ds(start, BF16_ROWS), :].astype(F32)[BF16_ROWS - SUBLANES:, :]
    return jnp.where(r0 > 0, rows, 0.0)


def _lru_kernel(x_ref, y_ref, cw_ref, cb_ref, wg_ref, bg_ref, lam_ref, o_ref, a_s, u_s):
    t = x_ref.shape[0]
    w = LRU_WIDTH
    seg = SUBLANES
    rows = t // seg
    neg_c_sp = -LRU_C * _softplus(-lam_ref[...])
    lanes = [slice(g * LANES, (g + 1) * LANES) for g in range(w // LANES)]

    def gates(ci, carry):
        r0 = pl.multiple_of(ci * rows, rows)
        cur = x_ref[pl.ds(r0, rows), :].astype(F32)
        xc = _causal_conv(cur, _prev_rows(x_ref, r0), cw_ref[...]) + cb_ref[...]
        ri = jnp.dot(xc.astype(BF16), wg_ref[...], preferred_element_type=F32) + bg_ref[...]
        r = _sigmoid(ri[:, :w])
        i = _sigmoid(ri[:, w:])
        log_a = r * neg_c_sp
        a = jnp.exp(log_a)
        gain2 = jnp.tanh(-log_a) * (1.0 + a * a)
        u = jnp.sqrt(gain2) * (i * xc)
        for g, ln in enumerate(lanes):
            a_s[g, pl.ds(ci, rows, stride=seg), :] = a[:, ln]
            u_s[g, pl.ds(ci, rows, stride=seg), :] = u[:, ln]
        return carry

    lax.fori_loop(0, seg, gates, 0)

    def local(r, carry):
        h, p = carry
        rs = pl.ds(pl.multiple_of(r * seg, seg), seg)
        a = jnp.concatenate([a_s[g, rs, :] for g in range(len(lanes))], axis=1)
        h = a * h + jnp.concatenate([u_s[g, rs, :] for g in range(len(lanes))], axis=1)
        p = a * p
        for g, ln in enumerate(lanes):
            u_s[g, rs, :] = h[:, ln]
            a_s[g, rs, :] = p[:, ln]
        return h, p

    h_fin, p_fin = lax.fori_loop(0, rows, local, (jnp.zeros((seg, w), F32), jnp.ones((seg, w), F32)),
                                 unroll=4)

    state = jnp.zeros((1, w), F32)
    incoming = []
    for s in range(seg):
        incoming.append(state)
        state = h_fin[s:s + 1, :] + p_fin[s:s + 1, :] * state

    for s in range(seg):
        def finish(rb, carry, s=s):
            src = pl.ds(rb * (seg * seg) + s, seg, stride=seg)
            dst = pl.ds(pl.multiple_of(s * rows + rb * seg, seg), seg)
            hl = jnp.concatenate([u_s[g, src, :] for g in range(len(lanes))], axis=1)
            pr = jnp.concatenate([a_s[g, src, :] for g in range(len(lanes))], axis=1)
            hs = hl + pr * incoming[s]
            o_ref[dst, :] = (hs * _gelu_tanh(y_ref[dst, :].astype(F32))).astype(o_ref.dtype)
            return carry

        lax.fori_loop(0, rows // seg, finish, 0, unroll=4)


def _rglru(proj, conv_w, conv_b, w_gates, b_gates, lam, l, batch, t):
    n = proj.shape[0]
    w = LRU_WIDTH
    const = lambda shape: _lspec(l, shape, lambda b: (0, 0))
    return pl.pallas_call(
        _lru_kernel,
        grid=(batch,),
        in_specs=[
            pl.BlockSpec((t, w), lambda b: (b, OFF_LX // w)),
            pl.BlockSpec((t, w), lambda b: (b, OFF_LY // w)),
            const((CONV_WIDTH, w)), const((1, w)), const((w, 2 * w)), const((1, 2 * w)), const((1, w)),
        ],
        out_specs=pl.BlockSpec((t, w), lambda b: (b, 0)),
        out_shape=jax.ShapeDtypeStruct((n, w), BF16),
        scratch_shapes=[pltpu.VMEM((w // LANES, t, LANES), F32), pltpu.VMEM((w // LANES, t, LANES), F32)],
        compiler_params=_params(("parallel",)),
        name="rglru",
    )(proj, proj, conv_w, conv_b, w_gates, b_gates, lam)


FFN_CHUNK = 256


def _mix_ffn_kernel(x_ref, g0_ref, g1_ref, g2_ref, bg_ref, dn_ref, mb_ref, lr_ref,
                    wdn_ref, wmb_ref, wlr_ref, wo_ref, gn_ref, wg_ref, wu_ref, wdown_ref, gf_ref,
                    o_ref, act_s, *, final_norm):
    d = D_MODEL
    bg = bg_ref[...]
    mixed = _sigmoid(g0_ref[...].astype(F32) + bg[:, 0:d]) * jnp.dot(
        dn_ref[...], wdn_ref[...], preferred_element_type=F32)
    mixed = mixed + _sigmoid(g1_ref[...].astype(F32) + bg[:, d:2 * d]) * jnp.dot(
        mb_ref[...], wmb_ref[...], preferred_element_type=F32)
    mixed = mixed + _sigmoid(g2_ref[...].astype(F32) + bg[:, 2 * d:3 * d]) * jnp.dot(
        lr_ref[...], wlr_ref[...], preferred_element_type=F32)
    x = x_ref[...] + jnp.dot(mixed.astype(BF16), wo_ref[...], preferred_element_type=F32)

    ms = jnp.mean(x * x, axis=-1, keepdims=True)
    hn = (x * lax.rsqrt(ms + NORM_EPS) * gn_ref[...]).astype(BF16)
    fc = FFN_CHUNK
    for ci in range(FFN_HIDDEN // fc):
        cols = slice(ci * fc, (ci + 1) * fc)
        gate = jnp.dot(hn, wg_ref[:, cols], preferred_element_type=F32)
        up = jnp.dot(hn, wu_ref[:, cols], preferred_element_type=F32)
        act_s[:, cols] = (_silu(gate) * up).astype(BF16)
    y = x + jnp.dot(act_s[...], wdown_ref[...], preferred_element_type=F32)
    if final_norm:
        ms = jnp.mean(y * y, axis=-1, keepdims=True)
        y = y * lax.rsqrt(ms + NORM_EPS) * gf_ref[...]
    o_ref[...] = y


def _mix_ffn(x2, proj, b_gate, y_dn, y_mb, y_lru, w_dn, w_mb, w_lru, w_o,
             gain, w_in, w_down, gain_final, l, tm, final_norm):
    n, d = x2.shape
    rowblk = lambda width, j: pl.BlockSpec((tm, width), lambda i: (i, j))
    const = lambda shape, j=0: _lspec(l, shape, lambda i: (0, j), single=True)
    g0 = OFF_GATES // d
    return pl.pallas_call(
        functools.partial(_mix_ffn_kernel, final_norm=final_norm),
        grid=(n // tm,),
        in_specs=[
            rowblk(d, 0), rowblk(d, g0), rowblk(d, g0 + 1), rowblk(d, g0 + 2), const((1, N_BRANCH * d)),
            rowblk(DN_WIDTH, 0), rowblk(MB_WIDTH, 0), rowblk(LRU_WIDTH, 0),
            const((DN_WIDTH, d)), const((MB_WIDTH, d)), const((LRU_WIDTH, d)), const((d, d)),
            const((1, d)),
            const((d, FFN_HIDDEN), 0),
            const((d, FFN_HIDDEN), 1),
            const((FFN_HIDDEN, d)),
            pl.BlockSpec((1, d), lambda i: (0, 0)),
        ],
        out_specs=rowblk(d, 0),
        out_shape=jax.ShapeDtypeStruct((n, d), F32),
        scratch_shapes=[pltpu.VMEM((tm, FFN_HIDDEN), BF16)],
        compiler_params=_params(("parallel",)),
        name="mix_ffn",
    )(x2, proj, proj, proj, b_gate, y_dn, y_mb, y_lru, w_dn, w_mb, w_lru, w_o,
      gain, w_in, w_in, w_down, gain_final)


def _rope_tables(t):
    half = ROPE_DIMS // 2
    inv_freq = ROPE_THETA ** (-jnp.arange(half, dtype=F32) / half)
    ang = jnp.arange(t, dtype=F32)[:, None] * inv_freq[None, :]
    cos, sin = jnp.cos(ang), jnp.sin(ang)
    pad = MB_HEAD_DIM - ROPE_DIMS
    cos_t = jnp.concatenate([cos, cos, jnp.ones((t, pad), F32)], axis=1)
    sin_t = jnp.concatenate([-sin, sin, jnp.zeros((t, pad), F32)], axis=1)
    return cos_t, sin_t


def _block_diag(w):
    nl, g, di, do = w.shape
    eye = jnp.eye(g, dtype=w.dtype)
    return (w[:, :, :, None, :] * eye[None, :, None, :, None]).reshape(nl, g * di, g * do)


def _pick_tile(n, pref):
    tile = min(pref, n)
    while n % tile:
        tile //= 2
    return tile


def kernel(x, norm_mix, w_in, b_gate, dn_conv, dn_a_log, dn_dt_bias, dn_norm, dn_out, mb_out, lru_conv_w, lru_conv_b, lru_w_r, lru_b_r, lru_w_i, lru_b_i, lru_lambda, lru_out, w_o, norm_ffn, ffn_in, ffn_down, norm_final):
    batch, t, d = x.shape
    assert d == D_MODEL and t % MB_BLOCK == 0 and t % DN_CHUNK == 0
    n = batch * t
    depth = w_in.shape[0]
    x2 = x.reshape(n, d)
    cos_t, sin_t = _rope_tables(t)
    tm_proj = _pick_tile(n, 1024)
    tm_mix = _pick_tile(n, 512)

    c_ab = 4 * DN_WIDTH
    c_mq = c_ab + 2 * DN_HEADS
    c_g = c_mq + 3 * MB_WIDTH + 2 * LRU_WIDTH

    row3 = lambda p: p[:, None, :]
    w_in_b = w_in.astype(BF16)
    w_main = jnp.concatenate([w_in_b[:, :, c_g:], w_in_b[:, :, :c_ab], w_in_b[:, :, c_mq:c_g]], axis=2)
    w_ab = jnp.pad(w_in_b[:, :, c_ab:c_mq], ((0, 0), (0, 0), (0, LANES - 2 * DN_HEADS)))
    alog_b = jnp.broadcast_to(dn_a_log[:, :, None, None], (depth, DN_HEADS, 1, LANES))
    dtb_b = jnp.broadcast_to(dn_dt_bias[:, :, None, None], (depth, DN_HEADS, 1, LANES))
    w_gates = jnp.concatenate([_block_diag(lru_w_r), _block_diag(lru_w_i)], axis=2).astype(BF16)
    b_gates = row3(jnp.concatenate([lru_b_r, lru_b_i], axis=1))
    w_dn, w_mb, w_lru, w_out = (w.astype(BF16) for w in (dn_out, mb_out, lru_out, w_o))
    w_ffn_in = ffn_in.astype(BF16)
    w_ffn_down = ffn_down.astype(BF16)

    for l in range(depth):
        proj, ab = _norm_proj(x2, row3(norm_mix), w_main, w_ab, l, tm_proj, PROJ_TN)
        y_dn = _deltanet(proj, ab, dn_conv, alog_b, dtb_b, row3(dn_norm), l, batch, t)
        y_mb = _moba(proj, cos_t, sin_t, batch, t)
        y_lru = _rglru(proj, lru_conv_w, row3(lru_conv_b), w_gates, b_gates, row3(lru_lambda), l, batch, t)
        x2 = _mix_ffn(x2, proj, row3(b_gate), y_dn, y_mb, y_lru, w_dn, w_mb, w_lru, w_out,
                      row3(norm_ffn), w_ffn_in, w_ffn_down, norm_final[None, :], l, tm_mix,
                      final_norm=(l == depth - 1))

    return x2.reshape(batch, t, d)
```

```python
import functools

import jax
import jax.numpy as jnp
from jax import lax
from jax.experimental import pallas as pl
from jax.experimental.pallas import tpu as pltpu

F32 = jnp.float32
BF16 = jnp.bfloat16
HIGHEST = lax.Precision.HIGHEST

D_MODEL = 1024
NORM_EPS = 1e-6
CONV_WIDTH = 4
DN_HEADS = 4
DN_HEAD_DIM = 128
DN_WIDTH = DN_HEADS * DN_HEAD_DIM
MB_HEADS = 4
MB_HEAD_DIM = 128
MB_WIDTH = MB_HEADS * MB_HEAD_DIM
MB_BLOCK = 256
MB_TOP_K = 3
MB_SCALE = MB_HEAD_DIM ** -0.5
LOG2_E = 1.4426950408889634
ROPE_THETA = 500000.0
ROPE_DIMS = MB_HEAD_DIM // 4
LRU_WIDTH = 512
LRU_GROUPS = 8
LRU_GROUP_DIM = LRU_WIDTH // LRU_GROUPS
LRU_C = 8.0
N_BRANCH = 3
FFN_HIDDEN = -(-8 * D_MODEL // (3 * 256)) * 256

LANES = 128
SUBLANES = 8
BF16_ROWS = 16
VMEM_LIMIT = 56 * 1024 * 1024

OFF_GATES = 0
OFF_DQ = OFF_GATES + N_BRANCH * D_MODEL
OFF_DK = OFF_DQ + DN_WIDTH
OFF_DV = OFF_DK + DN_WIDTH
OFF_DZ = OFF_DV + DN_WIDTH
OFF_MQ = OFF_DZ + DN_WIDTH
OFF_MK = OFF_MQ + MB_WIDTH
OFF_MV = OFF_MK + MB_WIDTH
OFF_LX = OFF_MV + MB_WIDTH
OFF_LY = OFF_LX + LRU_WIDTH
PROJ_COLS = OFF_LY + LRU_WIDTH
PROJ_TN = 3840

DN_CHUNK = 128
DN_INV_BLOCK = 16
DN_GROUP = 16
DN_POST_GROUP = 8


def _lspec(l, shape, index_map, single=False):
    mode = dict(pipeline_mode=pl.Buffered(1)) if single else {}
    return pl.BlockSpec((None,) + shape, lambda *g: (l,) + index_map(*g), **mode)


def _params(sem):
    return pltpu.CompilerParams(dimension_semantics=sem, vmem_limit_bytes=VMEM_LIMIT)


def _sigmoid(x):
    return 1.0 / (1.0 + jnp.exp(-x))


def _silu(x):
    return x * _sigmoid(x)


def _softplus(x):
    return jnp.maximum(x, 0.0) + jnp.log1p(jnp.exp(-jnp.abs(x)))


def _gelu_tanh(x):
    c = (2.0 / jnp.pi) ** 0.5
    return 0.5 * x * (1.0 + jnp.tanh(c * (x + 0.044715 * (x * x * x))))


def _bdot(a, b):
    return jnp.dot(a.astype(BF16), b.astype(BF16), preferred_element_type=F32)


def _bdot_nt(a, b):
    return lax.dot_general(a.astype(BF16), b.astype(BF16), (((1,), (1,)), ((), ())),
                           preferred_element_type=F32)


def _bdot_tn(a, b):
    return lax.dot_general(a.astype(BF16), b.astype(BF16), (((0,), (0,)), ((), ())),
                           preferred_element_type=F32)


def _norm_proj_kernel(x_ref, g_ref, w_ref, wab_ref, o_ref, oab_ref, hn_ref):
    @pl.when(pl.program_id(1) == 0)
    def _():
        x = x_ref[...]
        ms = jnp.mean(x * x, axis=-1, keepdims=True)
        hn = (x * lax.rsqrt(ms + NORM_EPS) * g_ref[...]).astype(BF16)
        hn_ref[...] = hn
        oab_ref[...] = jnp.dot(hn, wab_ref[...], preferred_element_type=F32)

    o_ref[...] = jnp.dot(hn_ref[...], w_ref[...], preferred_element_type=F32).astype(o_ref.dtype)


def _norm_proj(x2, gain, w_main, w_ab, l, tm, tn):
    n, d = x2.shape
    cols = w_main.shape[2]
    return pl.pallas_call(
        _norm_proj_kernel,
        grid=(n // tm, cols // tn),
        in_specs=[
            pl.BlockSpec((tm, d), lambda i, j: (i, 0)),
            _lspec(l, (1, d), lambda i, j: (0, 0)),
            _lspec(l, (d, tn), lambda i, j: (0, j)),
            _lspec(l, (d, LANES), lambda i, j: (0, 0)),
        ],
        out_specs=[
            pl.BlockSpec((tm, tn), lambda i, j: (i, j)),
            pl.BlockSpec((tm, LANES), lambda i, j: (i, 0)),
        ],
        out_shape=[
            jax.ShapeDtypeStruct((n, cols), BF16),
            jax.ShapeDtypeStruct((n, LANES), F32),
        ],
        scratch_shapes=[pltpu.VMEM((tm, d), BF16)],
        compiler_params=_params(("parallel", "arbitrary")),
        name="norm_proj",
    )(x2, gain, w_main, w_ab)


def _unit_lower_inverse(lows, row, col):
    c = lows[0].shape[0]
    same_blk = (row // DN_INV_BLOCK) == (col // DN_INV_BLOCK)
    eye = jnp.where(row == col, 1.0, 0.0).astype(F32)

    def neumann(ns, nilpotency):
        invs = [eye - n for n in ns]
        pws = [_bdot(n, n) for n in ns]
        levels = nilpotency.bit_length() - 2
        for lvl in range(levels):
            if lvl < levels - 1:
                both = [_bdot(pw, jnp.concatenate([pw, inv], axis=1)) for pw, inv in zip(pws, invs)]
                pws = [b[:, :c] for b in both]
                invs = [inv + b[:, c:] for inv, b in zip(invs, both)]
            else:
                invs = [inv + _bdot(pw, inv) for pw, inv in zip(pws, invs)]
        return invs

    dgs = [jnp.where(same_blk, low, 0.0) for low in lows]
    offs = [low - dg for low, dg in zip(lows, dgs)]
    inv_ds = neumann(dgs, DN_INV_BLOCK)
    inv_ms = neumann([_bdot(inv_d, off) for inv_d, off in zip(inv_ds, offs)], c // DN_INV_BLOCK)
    return [_bdot(inv_m, inv_d) for inv_m, inv_d in zip(inv_ms, inv_ds)]


def _dn_kernel(q_ref, k_ref, v_ref, z_ref, ab_ref, cwq_ref, cwk_ref, cwv_ref, alog_ref, dtb_ref, nw_ref,
               o_ref, sm_s, sb_s, qe_s, o0_s, st_s, egl_s):
    t = q_ref.shape[0]
    c = DN_CHUNK
    hd = DN_HEAD_DIM
    nc = t // c
    grp = min(DN_GROUP, nc)
    rows_g = grp * c
    h = pl.program_id(1)

    row = lax.broadcasted_iota(jnp.int32, (c, c), 0)
    col = lax.broadcasted_iota(jnp.int32, (c, c), 1)
    r2 = lax.broadcasted_iota(jnp.int32, (c, 2 * c), 0)
    c2 = lax.broadcasted_iota(jnp.int32, (c, 2 * c), 1)
    tril2 = jnp.where(r2 >= c2 % c, 1.0, 0.0).astype(BF16)
    pr = lax.broadcasted_iota(jnp.int32, (2 * LANES, 2 * LANES), 0) % LANES
    pc = lax.broadcasted_iota(jnp.int32, (2 * LANES, 2 * LANES), 1)
    pick2 = jnp.where(pr == jnp.where(pc < LANES, h, h + DN_HEADS), 1.0, 0.0).astype(BF16)
    sr = lax.broadcasted_iota(jnp.int32, ((CONV_WIDTH - 1) * c, 2 * c), 0)
    sc = lax.broadcasted_iota(jnp.int32, ((CONV_WIDTH - 1) * c, 2 * c), 1)
    shift_m = jnp.where(sc == c + sr % c - (sr // c + 1), 1.0, 0.0).astype(BF16)
    neg_a = -jnp.exp(alog_ref[...])
    dtb = dtb_ref[...]
    cw = jnp.concatenate([cwq_ref[...], cwk_ref[...], cwv_ref[...]], axis=1)

    def split2(x, axis):
        hi = x.astype(BF16)
        lo = (x - hi.astype(F32)).astype(BF16)
        return jnp.concatenate([hi, lo], axis=axis)

    def l2n(x):
        return x * lax.rsqrt(jnp.sum(x * x, axis=-1, keepdims=True) + NORM_EPS)

    def prep(gi, carry):
        r0 = pl.multiple_of(gi * rows_g, rows_g)
        rows = pl.ds(r0, rows_g)
        before = pl.ds(pl.multiple_of(jnp.maximum(r0 - c, 0), c), c)
        xg = jnp.concatenate([q_ref[rows, :], k_ref[rows, :], v_ref[rows, :]], axis=1).astype(BF16)
        xp = jnp.concatenate([q_ref[before, :], k_ref[before, :], v_ref[before, :]], axis=1).astype(BF16)
        xp = jnp.where(r0 > 0, xp, jnp.zeros_like(xp))
        ab2 = split2(ab_ref[rows, :], 1)
        agb = jnp.dot(ab2, pick2, preferred_element_type=F32)
        g_b = neg_a * _softplus(agb[:, :LANES] + dtb)
        beta_b = _sigmoid(agb[:, LANES:])
        rng = range(grp)
        sls = [slice(ci * c, (ci + 1) * c) for ci in rng]
        xcs = [xg[s] for s in sls]
        xxs = [jnp.concatenate([xp if ci == 0 else xcs[ci - 1], xcs[ci]], axis=0) for ci in rng]
        width = 3 * hd
        shs = []
        for ci in range(0, grp, 2):
            pair = jnp.dot(shift_m, jnp.concatenate(xxs[ci:ci + 2], axis=1), preferred_element_type=F32)
            shs += [pair[:, k * width:(k + 1) * width] for k in range(len(xxs[ci:ci + 2]))]
        last = CONV_WIDTH - 1
        convs = [xc.astype(F32) * cw[last:, :]
                 + sum(sh[(k - 1) * c:k * c] * cw[last - k:CONV_WIDTH - k, :] for k in range(1, CONV_WIDTH))
                 for xc, sh in zip(xcs, shs)]
        acts = [_silu(cv) for cv in convs]
        qcs = [l2n(a[:, :hd]) * (DN_HEAD_DIM ** -0.5) for a in acts]
        kcs = [l2n(a[:, hd:2 * hd]) for a in acts]
        vcs = [a[:, 2 * hd:] for a in acts]
        betas = [beta_b[s] for s in sls]
        gcbs = [jnp.dot(tril2, split2(g_b[s], 0), preferred_element_type=F32) for s in sls]
        gcrs = [gcb.T for gcb in gcbs]
        decays = [jnp.exp(jnp.where(row >= col, gcb - gcr, -jnp.inf)) for gcb, gcr in zip(gcbs, gcrs)]
        kbs = [kc * beta for kc, beta in zip(kcs, betas)]
        lows = [jnp.where(row > col, _bdot_nt(kb, kc) * decay, 0.0) for kb, kc, decay in zip(kbs, kcs, decays)]
        t_invs = _unit_lower_inverse(lows, row, col)
        egs = [jnp.exp(gcb) for gcb in gcbs]
        wus = [_bdot(t_inv, jnp.concatenate([kb * eg, vc * beta], axis=1))
               for t_inv, kb, eg, vc, beta in zip(t_invs, kbs, egs, vcs, betas)]
        attns = [_bdot_nt(qc, kc) * decay for qc, kc, decay in zip(qcs, kcs, decays)]
        gls = [gcb[c - 1:c, :] for gcb in gcbs]
        kes = [kc * jnp.exp(gl - gcb) for kc, gl, gcb in zip(kcs, gls, gcbs)]
        kws = [_bdot_tn(ke, wu) for ke, wu in zip(kes, wus)]
        aws = [_bdot(attn, wu) for attn, wu in zip(attns, wus)]
        for ci in rng:
            rs = pl.ds(pl.multiple_of(r0 + ci * c, c), c)
            sm_s[h, rs, :] = kws[ci][:, :hd].astype(BF16)
            sb_s[h, rs, :] = kws[ci][:, hd:]
            qe_s[h, rs, :] = (qcs[ci] * egs[ci] - aws[ci][:, :hd]).astype(BF16)
            o0_s[h, rs, :] = aws[ci][:, hd:]
            es = pl.ds(pl.multiple_of((gi * grp + ci) * SUBLANES, SUBLANES), SUBLANES)
            egl_s[h, es, :] = jnp.broadcast_to(jnp.exp(gls[ci]), (SUBLANES, LANES))
        return carry

    lax.fori_loop(0, nc // grp, prep, 0)

    @pl.when(h == DN_HEADS - 1)
    def _():
        heads = range(DN_HEADS)

        def step(ci, states):
            rs = pl.ds(pl.multiple_of(ci * c, c), c)
            es = pl.ds(pl.multiple_of(ci * SUBLANES, SUBLANES), SUBLANES)
            sbs = [s.astype(BF16) for s in states]
            for hh in heads:
                st_s[hh, rs, :] = sbs[hh]
            upd = [jnp.dot(sm_s[hh, rs, :], sbs[hh], preferred_element_type=F32) for hh in heads]
            return tuple(states[hh] * egl_s[hh, es, :][0:1, :] - upd[hh] + sb_s[hh, rs, :] for hh in heads)

        lax.fori_loop(0, nc, step, tuple(jnp.zeros((hd, hd), F32) for _ in heads), unroll=4)

        nw = nw_ref[...]
        pgrp = min(DN_POST_GROUP, nc)

        def post(gi, carry):
            r0 = pl.multiple_of(gi * pgrp * c, pgrp * c)
            rss = [pl.ds(pl.multiple_of(r0 + ci * c, c), c) for ci in range(pgrp)]
            for hh in heads:
                lanes = slice(hh * hd, (hh + 1) * hd)
                os_ = [jnp.dot(qe_s[hh, rs, :], st_s[hh, rs, :], preferred_element_type=F32) + o0_s[hh, rs, :]
                       for rs in rss]
                for rs, o in zip(rss, os_):
                    on = o * lax.rsqrt(jnp.mean(o * o, axis=-1, keepdims=True) + NORM_EPS) * nw
                    o_ref[rs, lanes] = (on * _silu(z_ref[rs, lanes].astype(F32))).astype(o_ref.dtype)
            return carry

        lax.fori_loop(0, nc // pgrp, post, 0)


def _deltanet(proj, ab, conv_w, alog_b, dtb_b, norm_w, l, batch, t):
    n = proj.shape[0]
    hb = DN_HEAD_DIM
    qb, kb_, vb_ = OFF_DQ // hb, OFF_DK // hb, OFF_DV // hb
    col = lambda base: pl.BlockSpec((t, hb), lambda b, h: (b, base + h))
    cw = lambda base: _lspec(l, (CONV_WIDTH, hb), lambda b, h: (0, base + h))
    par = _lspec(l, (None, 1, LANES), lambda b, h: (h, 0, 0))
    return pl.pallas_call(
        _dn_kernel,
        grid=(batch, DN_HEADS),
        in_specs=[
            col(qb), col(kb_), col(vb_),
            pl.BlockSpec((t, DN_WIDTH), lambda b, h: (b, OFF_DZ // DN_WIDTH)),
            pl.BlockSpec((t, LANES), lambda b, h: (b, 0)),
            cw(0), cw(DN_HEADS), cw(2 * DN_HEADS),
            par, par,
            _lspec(l, (1, hb), lambda b, h: (0, 0)),
        ],
        out_specs=pl.BlockSpec((t, DN_WIDTH), lambda b, h: (b, 0)),
        out_shape=jax.ShapeDtypeStruct((n, DN_WIDTH), BF16),
        scratch_shapes=[
            pltpu.VMEM((DN_HEADS, t, hb), BF16),
            pltpu.VMEM((DN_HEADS, t, hb), F32),
            pltpu.VMEM((DN_HEADS, t, hb), BF16),
            pltpu.VMEM((DN_HEADS, t, hb), F32),
            pltpu.VMEM((DN_HEADS, t, hb), BF16),
            pltpu.VMEM((DN_HEADS, t // DN_CHUNK * SUBLANES, LANES), F32),
        ],
        compiler_params=_params(("parallel", "arbitrary")),
        name="deltanet",
    )(proj, proj, proj, proj, ab, conv_w, conv_w, conv_w, alog_b, dtb_b, norm_w)


def _rope(x, cos_t, sin_t):
    half = ROPE_DIMS // 2
    lane = lax.broadcasted_iota(jnp.int32, x.shape, 1)
    swapped = jnp.where(lane < half, pltpu.roll(x, LANES - half, axis=1), pltpu.roll(x, half, axis=1))
    return x * cos_t + swapped * sin_t


MB_Q_GROUP = 8


def _moba_kernel(q_ref, k_ref, v_ref, cos_ref, sin_ref, o_ref, kr_s, vt_s, qs_s):
    t = k_ref.shape[0]
    blk = MB_BLOCK
    nb = t // blk
    blocks = [slice(j * blk, (j + 1) * blk) for j in range(nb)]

    kms = []
    for sl in blocks:
        kr = _rope(k_ref[sl, :].astype(F32), cos_ref[sl, :], sin_ref[sl, :])
        kr_s[sl, :] = kr.astype(BF16)
        kms.append(jnp.mean(kr, axis=0, keepdims=True))
        vt_s[:, sl] = v_ref[sl, :].astype(F32).T.astype(BF16)
    km = jnp.concatenate(kms, axis=0)

    bidx = lax.broadcasted_iota(jnp.int32, (nb, blk), 0)
    sels = []
    for i, sl in enumerate(blocks):
        q = _rope(q_ref[sl, :].astype(F32), cos_ref[sl, :], sin_ref[sl, :])
        qs_s[sl, :] = (q * (MB_SCALE * LOG2_E)).astype(BF16)
        g_t = lax.dot_general(km, q, (((1,), (1,)), ((), ())), precision=HIGHEST,
                              preferred_element_type=F32)
        rank = jnp.zeros((nb, blk), F32)
        for m in range(i):
            gm = g_t[m:m + 1, :]
            rank = rank + jnp.where(gm > g_t, 1.0, jnp.where((gm == g_t) & (bidx > m), 1.0, 0.0))
        sels.append(jnp.where((rank < MB_TOP_K) & (bidx < i), 1.0, 0.0))

    krow = lax.broadcasted_iota(jnp.int32, (blk, blk), 0)
    qcol = lax.broadcasted_iota(jnp.int32, (blk, blk), 1)

    for g0 in range(0, nb, MB_Q_GROUP):
        tiles = list(range(g0, min(g0 + MB_Q_GROUP, nb)))
        ss = {}
        for i in tiles:
            for j in range(i + 1):
                s = lax.dot_general(kr_s[blocks[j], :], qs_s[blocks[i], :], (((1,), (1,)), ((), ())),
                                    preferred_element_type=F32)
                mask = (krow <= qcol) if j == i else (sels[i][j:j + 1, :] > 0.5)
                ss[i, j] = jnp.where(mask, s, -jnp.inf)
        m_all = {i: functools.reduce(jnp.maximum, [jnp.max(ss[i, j], axis=0, keepdims=True)
                                                   for j in range(i + 1)]) for i in tiles}
        ps = {(i, j): jnp.exp2(ss[i, j] - m_all[i]) for (i, j) in ss}
        for i in tiles:
            l_all = functools.reduce(jnp.add, [jnp.sum(ps[i, j], axis=0, keepdims=True) for j in range(i + 1)])
            p_all = jnp.concatenate([ps[i, j].astype(BF16) for j in range(i + 1)], axis=0)
            acc = jnp.dot(vt_s[:, :(i + 1) * blk], p_all, preferred_element_type=F32)
            o_ref[blocks[i], :] = (acc / l_all).T.astype(o_ref.dtype)


def _moba(proj, cos_t, sin_t, batch, t):
    n = proj.shape[0]
    hb = MB_HEAD_DIM
    qb, kb_, vb_ = OFF_MQ // hb, OFF_MK // hb, OFF_MV // hb
    full = lambda base: pl.BlockSpec((t, hb), lambda b, h: (b, base + h))
    table = pl.BlockSpec((t, hb), lambda b, h: (0, 0))
    return pl.pallas_call(
        _moba_kernel,
        grid=(batch, MB_HEADS),
        in_specs=[full(qb), full(kb_), full(vb_), table, table],
        out_specs=pl.BlockSpec((t, hb), lambda b, h: (b, h)),
        out_shape=jax.ShapeDtypeStruct((n, MB_WIDTH), BF16),
        scratch_shapes=[
            pltpu.VMEM((t, hb), BF16),
            pltpu.VMEM((hb, t), BF16),
            pltpu.VMEM((t, hb), BF16),
        ],
        compiler_params=_params(("parallel", "parallel")),
        name="moba",
    )(proj, proj, proj, cos_t, sin_t)


def _causal_conv(cur, prev, w):
    ext = jnp.concatenate([prev, cur], axis=0)
    out = cur * w[CONV_WIDTH - 1:CONV_WIDTH, :]
    for s in range(1, CONV_WIDTH):
        shifted = pltpu.roll(ext, s, axis=0)[SUBLANES:, :]
        out = out + shifted * w[CONV_WIDTH - 1 - s:CONV_WIDTH - s, :]
    return out


def _prev_rows(ref, r0):
    start = pl.multiple_of(jnp.maximum(r0 - BF16_ROWS, 0), BF16_ROWS)
    rows = ref[pl.ds(start, BF16_ROWS), :].astype(F32)[BF16_ROWS - SUBLANES:, :]
    return jnp.where(r0 > 0, rows, 0.0)


def _lru_kernel(x_ref, y_ref, cw_ref, cb_ref, wg_ref, bg_ref, lam_ref, o_ref, a_s, u_s, wbd_s):
    t = x_ref.shape[0]
    w = LRU_WIDTH
    seg = SUBLANES
    rows = t // seg
    neg_c_sp = -LRU_C * _softplus(-lam_ref[...])
    lanes = [slice(g * LANES, (g + 1) * LANES) for g in range(w // LANES)]

    gd = LRU_GROUP_DIM
    rg = lax.broadcasted_iota(jnp.int32, (w, 2 * w), 0) // gd
    cg = (lax.broadcasted_iota(jnp.int32, (w, 2 * w), 1) % w) // gd
    tiled = jnp.concatenate([wg_ref[...]] * LRU_GROUPS, axis=0)
    wbd_s[...] = jnp.where(rg == cg, tiled, jnp.zeros_like(tiled))

    def gates(ci, carry):
        r0 = pl.multiple_of(ci * rows, rows)
        cur = x_ref[pl.ds(r0, rows), :].astype(F32)
        xc = _causal_conv(cur, _prev_rows(x_ref, r0), cw_ref[...]) + cb_ref[...]
        ri = jnp.dot(xc.astype(BF16), wbd_s[...], preferred_element_type=F32) + bg_ref[...]
        r = _sigmoid(ri[:, :w])
        i = _sigmoid(ri[:, w:])
        log_a = r * neg_c_sp
        a = jnp.exp(log_a)
        gain2 = jnp.tanh(-log_a) * (1.0 + a * a)
        u = jnp.sqrt(gain2) * (i * xc)
        for g, ln in enumerate(lanes):
            a_s[g, pl.ds(ci, rows, stride=seg), :] = a[:, ln]
            u_s[g, pl.ds(ci, rows, stride=seg), :] = u[:, ln]
        return carry

    lax.fori_loop(0, seg, gates, 0)

    def local(r, carry):
        h, p = carry
        rs = pl.ds(pl.multiple_of(r * seg, seg), seg)
        a = jnp.concatenate([a_s[g, rs, :] for g in range(len(lanes))], axis=1)
        h = a * h + jnp.concatenate([u_s[g, rs, :] for g in range(len(lanes))], axis=1)
        p = a * p
        for g, ln in enumerate(lanes):
            u_s[g, rs, :] = h[:, ln]
            a_s[g, rs, :] = p[:, ln]
        return h, p

    h_fin, p_fin = lax.fori_loop(0, rows, local, (jnp.zeros((seg, w), F32), jnp.ones((seg, w), F32)),
                                 unroll=4)

    state = jnp.zeros((1, w), F32)
    incoming = []
    for s in range(seg):
        incoming.append(state)
        state = h_fin[s:s + 1, :] + p_fin[s:s + 1, :] * state

    for s in range(seg):
        def finish(rb, carry, s=s):
            src = pl.ds(rb * (seg * seg) + s, seg, stride=seg)
            dst = pl.ds(pl.multiple_of(s * rows + rb * seg, seg), seg)
            hl = jnp.concatenate([u_s[g, src, :] for g in range(len(lanes))], axis=1)
            pr = jnp.concatenate([a_s[g, src, :] for g in range(len(lanes))], axis=1)
            hs = hl + pr * incoming[s]
            o_ref[dst, :] = (hs * _gelu_tanh(y_ref[dst, :].astype(F32))).astype(o_ref.dtype)
            return carry

        lax.fori_loop(0, rows // seg, finish, 0, unroll=4)


def _rglru(proj, conv_w, conv_b, w_gates, b_gates, lam, l, batch, t):
    n = proj.shape[0]
    w = LRU_WIDTH
    const = lambda shape: _lspec(l, shape, lambda b: (0, 0))
    return pl.pallas_call(
        _lru_kernel,
        grid=(batch,),
        in_specs=[
            pl.BlockSpec((t, w), lambda b: (b, OFF_LX // w)),
            pl.BlockSpec((t, w), lambda b: (b, OFF_LY // w)),
            const((CONV_WIDTH, w)), const((1, w)), const((LRU_GROUP_DIM, 2 * w)), const((1, 2 * w)), const((1, w)),
        ],
        out_specs=pl.BlockSpec((t, w), lambda b: (b, 0)),
        out_shape=jax.ShapeDtypeStruct((n, w), BF16),
        scratch_shapes=[pltpu.VMEM((w // LANES, t, LANES), F32), pltpu.VMEM((w // LANES, t, LANES), F32),
                        pltpu.VMEM((w, 2 * w), BF16)],
        compiler_params=_params(("parallel",)),
        name="rglru",
    )(proj, proj, conv_w, conv_b, w_gates, b_gates, lam)


FFN_CHUNK = 256


def _mix_ffn_kernel(x_ref, g0_ref, g1_ref, g2_ref, bg_ref, dn_ref, mb_ref, lr_ref,
                    wdn_ref, wmb_ref, wlr_ref, wo_ref, gn_ref, wg_ref, wu_ref, wdown_ref, gf_ref,
                    o_ref, act_s, *, final_norm):
    d = D_MODEL
    bg = bg_ref[...]
    mixed = _sigmoid(g0_ref[...].astype(F32) + bg[:, 0:d]) * jnp.dot(
        dn_ref[...], wdn_ref[...], preferred_element_type=F32)
    mixed = mixed + _sigmoid(g1_ref[...].astype(F32) + bg[:, d:2 * d]) * jnp.dot(
        mb_ref[...], wmb_ref[...], preferred_element_type=F32)
    mixed = mixed + _sigmoid(g2_ref[...].astype(F32) + bg[:, 2 * d:3 * d]) * jnp.dot(
        lr_ref[...], wlr_ref[...], preferred_element_type=F32)
    x = x_ref[...] + jnp.dot(mixed.astype(BF16), wo_ref[...], preferred_element_type=F32)

    ms = jnp.mean(x * x, axis=-1, keepdims=True)
    hn = (x * lax.rsqrt(ms + NORM_EPS) * gn_ref[...]).astype(BF16)
    fc = FFN_CHUNK
    for ci in range(FFN_HIDDEN // fc):
        cols = slice(ci * fc, (ci + 1) * fc)
        gate = jnp.dot(hn, wg_ref[:, cols], preferred_element_type=F32)
        up = jnp.dot(hn, wu_ref[:, cols], preferred_element_type=F32)
        act_s[:, cols] = (_silu(gate) * up).astype(BF16)
    y = x + jnp.dot(act_s[...], wdown_ref[...], preferred_element_type=F32)
    if final_norm:
        ms = jnp.mean(y * y, axis=-1, keepdims=True)
        y = y * lax.rsqrt(ms + NORM_EPS) * gf_ref[...]
    o_ref[...] = y


def _mix_ffn(x2, proj, b_gate, y_dn, y_mb, y_lru, w_dn, w_mb, w_lru, w_o,
             gain, w_in, w_down, gain_final, l, tm, final_norm):
    n, d = x2.shape
    rowblk = lambda width, j: pl.BlockSpec((tm, width), lambda i: (i, j))
    const = lambda shape, j=0: _lspec(l, shape, lambda i: (0, j), single=True)
    g0 = OFF_GATES // d
    return pl.pallas_call(
        functools.partial(_mix_ffn_kernel, final_norm=final_norm),
        grid=(n // tm,),
        in_specs=[
            rowblk(d, 0), rowblk(d, g0), rowblk(d, g0 + 1), rowblk(d, g0 + 2), const((1, N_BRANCH * d)),
            rowblk(DN_WIDTH, 0), rowblk(MB_WIDTH, 0), rowblk(LRU_WIDTH, 0),
            const((DN_WIDTH, d)), const((MB_WIDTH, d)), const((LRU_WIDTH, d)), const((d, d)),
            const((1, d)),
            const((d, FFN_HIDDEN), 0),
            const((d, FFN_HIDDEN), 1),
            const((FFN_HIDDEN, d)),
            pl.BlockSpec((1, d), lambda i: (0, 0)),
        ],
        out_specs=rowblk(d, 0),
        out_shape=jax.ShapeDtypeStruct((n, d), F32),
        scratch_shapes=[pltpu.VMEM((tm, FFN_HIDDEN), BF16)],
        compiler_params=_params(("parallel",)),
        name="mix_ffn",
    )(x2, proj, proj, proj, b_gate, y_dn, y_mb, y_lru, w_dn, w_mb, w_lru, w_o,
      gain, w_in, w_in, w_down, gain_final)


def _rope_tables(t):
    half = ROPE_DIMS // 2
    inv_freq = ROPE_THETA ** (-jnp.arange(half, dtype=F32) / half)
    ang = jnp.arange(t, dtype=F32)[:, None] * inv_freq[None, :]
    cos, sin = jnp.cos(ang), jnp.sin(ang)
    pad = MB_HEAD_DIM - ROPE_DIMS
    cos_t = jnp.concatenate([cos, cos, jnp.ones((t, pad), F32)], axis=1)
    sin_t = jnp.concatenate([-sin, sin, jnp.zeros((t, pad), F32)], axis=1)
    return cos_t, sin_t


def _group_cols(w):
    nl, g, di, do = w.shape
    return w.transpose(0, 2, 1, 3).reshape(nl, di, g * do)


def _pick_tile(n, pref):
    tile = min(pref, n)
    while n % tile:
        tile //= 2
    return tile


def kernel(x, norm_mix, w_in, b_gate, dn_conv, dn_a_log, dn_dt_bias, dn_norm, dn_out, mb_out, lru_conv_w, lru_conv_b, lru_w_r, lru_b_r, lru_w_i, lru_b_i, lru_lambda, lru_out, w_o, norm_ffn, ffn_in, ffn_down, norm_final):
    batch, t, d = x.shape
    assert d == D_MODEL and t % MB_BLOCK == 0 and t % DN_CHUNK == 0
    n = batch * t
    depth = w_in.shape[0]
    x2 = x.reshape(n, d)
    cos_t, sin_t = _rope_tables(t)
    tm_proj = _pick_tile(n, 1024)
    tm_mix = _pick_tile(n, 512)

    c_ab = 4 * DN_WIDTH
    c_mq = c_ab + 2 * DN_HEADS
    c_g = c_mq + 3 * MB_WIDTH + 2 * LRU_WIDTH

    row3 = lambda p: p[:, None, :]
    w_in_b = w_in.astype(BF16)
    w_main = jnp.concatenate([w_in_b[:, :, c_g:], w_in_b[:, :, :c_ab], w_in_b[:, :, c_mq:c_g]], axis=2)
    w_ab = jnp.pad(w_in_b[:, :, c_ab:c_mq], ((0, 0), (0, 0), (0, LANES - 2 * DN_HEADS)))
    alog_b = jnp.broadcast_to(dn_a_log[:, :, None, None], (depth, DN_HEADS, 1, LANES))
    dtb_b = jnp.broadcast_to(dn_dt_bias[:, :, None, None], (depth, DN_HEADS, 1, LANES))
    w_gates = jnp.concatenate([_group_cols(lru_w_r), _group_cols(lru_w_i)], axis=2).astype(BF16)
    b_gates = row3(jnp.concatenate([lru_b_r, lru_b_i], axis=1))
    w_dn, w_mb, w_lru, w_out = (w.astype(BF16) for w in (dn_out, mb_out, lru_out, w_o))
    w_ffn_in = ffn_in.astype(BF16)
    w_ffn_down = ffn_down.astype(BF16)

    for l in range(depth):
        proj, ab = _norm_proj(x2, row3(norm_mix), w_main, w_ab, l, tm_proj, PROJ_TN)
        y_dn = _deltanet(proj, ab, dn_conv, alog_b, dtb_b, row3(dn_norm), l, batch, t)
        y_mb = _moba(proj, cos_t, sin_t, batch, t)
        y_lru = _rglru(proj, lru_conv_w, row3(lru_conv_b), w_gates, b_gates, row3(lru_lambda), l, batch, t)
        x2 = _mix_ffn(x2, proj, row3(b_gate), y_dn, y_mb, y_lru, w_dn, w_mb, w_lru, w_out,
                      row3(norm_ffn), w_ffn_in, w_ffn_down, norm_final[None, :], l, tm_mix,
                      final_norm=(l == depth - 1))

    return x2.reshape(batch, t, d)
```
